```python
import math
import jax, jax.numpy as jnp
from jax import lax
import numpy as np

D_MODEL = 1024
BATCH = 1
SEQ = 16384
DEPTH = 4

N_MIXERS = 2
N_HEADS = 16
HEAD_DIM = D_MODEL // N_HEADS
ATTN_WIDTH = N_HEADS * HEAD_DIM
Q_BLOCK = 128
CONV_WIDTH = D_MODEL
CONV_TAPS = 3
PLE_DIM = 256
N_ATTN_LAYERS = (DEPTH + 1) // 2
N_CONV_LAYERS = DEPTH // 2
ATTN_IN_COLS = 4 * ATTN_WIDTH + N_HEADS
CONV_IN_COLS = 4 * CONV_WIDTH
RMS_EPS = 1e-6

kernel_name = "fox_shortconv_interleaved_hybrid"


def rmsnorm(x, g):
    xf = x.astype(jnp.float32)
    y = xf * lax.rsqrt(jnp.mean(xf * xf, axis=-1, keepdims=True) + RMS_EPS)
    return (y * g.astype(jnp.float32)).astype(x.dtype)


def fox_attention(q, k, v, log_f):
    b, s, h, dh = q.shape
    nb = s // Q_BLOCK
    scale = 1.0 / math.sqrt(dh)
    c = jnp.cumsum(log_f, axis=1)
    c_keys = jnp.transpose(c, (0, 2, 1))
    q_blocks = jnp.transpose(q.reshape(b, nb, Q_BLOCK, h, dh), (1, 0, 2, 3, 4))
    c_blocks = jnp.transpose(c.reshape(b, nb, Q_BLOCK, h), (1, 0, 3, 2))
    starts = jnp.arange(nb, dtype=jnp.int32) * Q_BLOCK
    k_pos = jnp.arange(s, dtype=jnp.int32)

    def one_block(args):
        qi, ci, start = args
        logits = jnp.einsum('bqhd,bkhd->bhqk', qi, k).astype(jnp.float32) * scale
        logits = logits + ci[..., :, None] - c_keys[:, :, None, :]
        q_pos = start + jnp.arange(Q_BLOCK, dtype=jnp.int32)
        causal = k_pos[None, :] <= q_pos[:, None]
        logits = jnp.where(causal[None, None], logits, -jnp.inf)
        probs = jax.nn.softmax(logits, axis=-1)
        return jnp.einsum('bhqk,bkhd->bqhd', probs.astype(v.dtype), v)

    out = lax.map(one_block, (q_blocks, c_blocks, starts))
    return jnp.transpose(out, (1, 0, 2, 3, 4)).reshape(b, s, h, dh)


def fox_mixer(h, w_in, b_f, w_out):
    b, s, _ = h.shape
    proj = h @ w_in
    q = proj[..., 0 * ATTN_WIDTH:1 * ATTN_WIDTH].reshape(b, s, N_HEADS, HEAD_DIM)
    k = proj[..., 1 * ATTN_WIDTH:2 * ATTN_WIDTH].reshape(b, s, N_HEADS, HEAD_DIM)
    v = proj[..., 2 * ATTN_WIDTH:3 * ATTN_WIDTH].reshape(b, s, N_HEADS, HEAD_DIM)
    z = proj[..., 3 * ATTN_WIDTH:4 * ATTN_WIDTH]
    f_logit = proj[..., 4 * ATTN_WIDTH:].astype(jnp.float32) + b_f.astype(jnp.float32)
    log_f = jax.nn.log_sigmoid(f_logit)
    o = fox_attention(q, k, v, log_f).reshape(b, s, ATTN_WIDTH)
    return (o * jax.nn.silu(z)) @ w_out


def causal_depthwise_conv(u, w):
    s = u.shape[1]
    u_pad = jnp.pad(u, ((0, 0), (CONV_TAPS - 1, 0), (0, 0)))
    y = u_pad[:, 0:s] * w[0]
    for tap in range(1, CONV_TAPS):
        y = y + u_pad[:, tap:tap + s] * w[tap]
    return y


def conv_mixer(h, w_in, conv_w, w_out):
    proj = h @ w_in
    gate_b = proj[..., 0 * CONV_WIDTH:1 * CONV_WIDTH]
    gate_c = proj[..., 1 * CONV_WIDTH:2 * CONV_WIDTH]
    h_in = proj[..., 2 * CONV_WIDTH:3 * CONV_WIDTH]
    z = proj[..., 3 * CONV_WIDTH:4 * CONV_WIDTH]
    y = gate_b * causal_depthwise_conv(gate_c * h_in, conv_w)
    return (y * jax.nn.silu(z)) @ w_out


def setup_inputs(seed: int = 0) -> dict:
    key = jax.random.key(seed)
    ks = jax.random.split(key, 16)
    f32 = jnp.float32
    d = D_MODEL
    x = jax.random.normal(ks[0], (BATCH, SEQ, d), f32)
    p = jax.random.normal(ks[1], (DEPTH, BATCH, SEQ, PLE_DIM), f32)
    attn_w_in = jax.random.normal(ks[2], (N_ATTN_LAYERS, d, ATTN_IN_COLS), f32) * d ** -0.5
    attn_b_f = jax.random.uniform(ks[3], (N_ATTN_LAYERS, N_HEADS), f32, 1.0, 4.0)
    attn_w_out = jax.random.normal(ks[4], (N_ATTN_LAYERS, ATTN_WIDTH, d), f32) * ATTN_WIDTH ** -0.5
    conv_w_in = jax.random.normal(ks[5], (N_CONV_LAYERS, d, CONV_IN_COLS), f32) * d ** -0.5
    conv_kernel = jax.random.normal(ks[6], (N_CONV_LAYERS, CONV_TAPS, CONV_WIDTH), f32) * CONV_TAPS ** -0.5
    conv_w_out = jax.random.normal(ks[7], (N_CONV_LAYERS, CONV_WIDTH, d), f32) * CONV_WIDTH ** -0.5
    norm_pre = 1.0 + 0.05 * jax.random.normal(ks[8], (DEPTH, d), f32)
    norm_post = 1.0 + 0.05 * jax.random.normal(ks[9], (DEPTH, d), f32)
    pe_norm = 1.0 + 0.05 * jax.random.normal(ks[10], (DEPTH, d), f32)
    pe_w_gate = jax.random.normal(ks[11], (DEPTH, d, d), f32) * d ** -0.5
    pe_w_proj = jax.random.normal(ks[12], (DEPTH, PLE_DIM, d), f32) * PLE_DIM ** -0.5
    return {"x": x, "p": p,
            "attn_w_in": attn_w_in, "attn_b_f": attn_b_f, "attn_w_out": attn_w_out,
            "conv_w_in": conv_w_in, "conv_kernel": conv_kernel, "conv_w_out": conv_w_out,
            "norm_pre": norm_pre, "norm_post": norm_post,
            "pe_norm": pe_norm, "pe_w_gate": pe_w_gate, "pe_w_proj": pe_w_proj}


def reference(x, p, attn_w_in, attn_b_f, attn_w_out, conv_w_in, conv_kernel, conv_w_out,
              norm_pre, norm_post, pe_norm, pe_w_gate, pe_w_proj):
    for i in range(DEPTH):
        h = rmsnorm(x, norm_pre[i])
        j = i // N_MIXERS
        if i % N_MIXERS == 0:
            y = fox_mixer(h, attn_w_in[j], attn_b_f[j], attn_w_out[j])
        else:
            y = conv_mixer(h, conv_w_in[j], conv_kernel[j], conv_w_out[j])
        x = x + rmsnorm(y, norm_post[i])
        gate = jax.nn.sigmoid(rmsnorm(x, pe_norm[i]) @ pe_w_gate[i])
        x = x + gate * (p[i] @ pe_w_proj[i])
    return x
```

```python
import functools
import math

import jax
import jax.numpy as jnp
from jax import lax
from jax.experimental import pallas as pl
from jax.experimental.pallas import tpu as pltpu

N_HEADS = 16
HEAD_DIM = 64
CONV_TAPS = 3
RMS_EPS = 1e-6
LOG2E = math.log2(math.e)

LANES = 128
AUG_ROWS = 16
VT_ROWS = HEAD_DIM + AUG_ROWS
VMEM_LIMIT_BYTES = 56 * 1024 * 1024
MASK_VALUE = -1e30

f32 = jnp.float32
bf16 = jnp.bfloat16


def _rmsnorm(x, g):
    ms = jnp.mean(x * x, axis=-1, keepdims=True)
    return x * lax.rsqrt(ms + RMS_EPS) * g


def _split3(c):
    hi = c.astype(bf16).astype(f32)
    r = c - hi
    mid = r.astype(bf16).astype(f32)
    lo = (r - mid).astype(bf16).astype(f32)
    return hi, mid, lo


def _log_sigmoid(x):
    return jnp.minimum(x, 0.0) - jnp.log1p(jnp.exp(-jnp.abs(x)))


def _const_spec(shape):
    return pl.BlockSpec(shape, lambda *_: (0,) * len(shape))


def _attn_in_kernel(x_ref, g_ref, w_ref, wf_ref, bf_ref,
                    qta_ref, ka_ref, vta_ref, gate_ref, carry_ref, *, ts, d):
    @pl.when(pl.program_id(0) == 0)
    def _():
        carry_ref[...] = jnp.zeros_like(carry_ref)

    h = _rmsnorm(x_ref[...], g_ref[...]).astype(bf16)
    proj = jnp.dot(h, w_ref[...], preferred_element_type=f32)
    f_logit = jnp.dot(h, wf_ref[...], preferred_element_type=f32) + bf_ref[...]
    logf2 = _log_sigmoid(f_logit) * LOG2E

    rows = lax.broadcasted_iota(jnp.int32, (ts, LANES), 0)
    c = logf2
    shift = 1
    while shift < ts:
        c = c + jnp.where(rows >= shift, pltpu.roll(c, shift, axis=0), 0.0)
        shift *= 2
    c = c + carry_ref[...]
    carry_ref[...] = c[ts - 1:ts, :]
    c_t = c.T

    q_scale = LOG2E / math.sqrt(HEAD_DIM)
    lane = lax.broadcasted_iota(jnp.int32, (ts, LANES), 1)
    row = lax.broadcasted_iota(jnp.int32, (HEAD_DIM, ts), 0)
    vrow = lax.broadcasted_iota(jnp.int32, (AUG_ROWS, ts), 0)
    ones_blk = jnp.where(vrow == 0, 1.0, 0.0).astype(bf16)
    for pair in range(N_HEADS // 2):
        lo_col = pair * LANES
        q_t = (proj[:, lo_col:lo_col + LANES] * q_scale).T
        k_pair = proj[:, d + lo_col:d + lo_col + LANES]
        k_swap = pltpu.roll(k_pair, HEAD_DIM, axis=1)
        v_t = proj[:, 2 * d + lo_col:2 * d + lo_col + LANES].T
        for e in range(2):
            hd = 2 * pair + e
            hi, mid, lo = _split3(jnp.broadcast_to(c[:, hd:hd + 1], (ts, LANES)))
            k_h = k_pair if e == 0 else k_swap
            ka = jnp.where(lane < HEAD_DIM, k_h,
                 jnp.where(lane < HEAD_DIM + 3, 1.0,
                 jnp.where(lane == HEAD_DIM + 3, -hi,
                 jnp.where(lane == HEAD_DIM + 4, -mid,
                 jnp.where(lane == HEAD_DIM + 5, -lo, 0.0)))))
            ka_ref[:, hd * LANES:(hd + 1) * LANES] = ka.astype(bf16)
            hi, mid, lo = _split3(jnp.broadcast_to(c_t[hd:hd + 1, :], (HEAD_DIM, ts)))
            aug = jnp.where(row == 0, hi,
                  jnp.where(row == 1, mid,
                  jnp.where(row == 2, lo,
                  jnp.where(row < 6, 1.0, 0.0))))
            base = hd * 2 * HEAD_DIM
            qta_ref[base:base + HEAD_DIM, :] = q_t[e * HEAD_DIM:(e + 1) * HEAD_DIM].astype(bf16)
            qta_ref[base + HEAD_DIM:base + 2 * HEAD_DIM, :] = aug.astype(bf16)
            vbase = hd * VT_ROWS
            vta_ref[vbase:vbase + HEAD_DIM, :] = v_t[e * HEAD_DIM:(e + 1) * HEAD_DIM].astype(bf16)
            vta_ref[vbase + HEAD_DIM:vbase + VT_ROWS, :] = ones_blk
    z = proj[:, 3 * d:4 * d]
    gate_ref[...] = (z * jax.nn.sigmoid(z)).astype(bf16)


def _attn_in(x, g, w_main, w_f, b_f, *, ts):
    s, d = x.shape
    kern = functools.partial(_attn_in_kernel, ts=ts, d=d)
    return pl.pallas_call(
        kern,
        grid=(s // ts,),
        in_specs=[
            pl.BlockSpec((ts, d), lambda i: (i, 0)),
            _const_spec((1, d)),
            _const_spec((d, 4 * d)),
            _const_spec((d, LANES)),
            _const_spec((1, LANES)),
        ],
        out_specs=[
            pl.BlockSpec((N_HEADS * 2 * HEAD_DIM, ts), lambda i: (0, i)),
            pl.BlockSpec((ts, N_HEADS * LANES), lambda i: (i, 0)),
            pl.BlockSpec((N_HEADS * VT_ROWS, ts), lambda i: (0, i)),
            pl.BlockSpec((ts, d), lambda i: (i, 0)),
        ],
        out_shape=[
            jax.ShapeDtypeStruct((N_HEADS * 2 * HEAD_DIM, s), bf16),
            jax.ShapeDtypeStruct((s, N_HEADS * LANES), bf16),
            jax.ShapeDtypeStruct((N_HEADS * VT_ROWS, s), bf16),
            jax.ShapeDtypeStruct((s, d), bf16),
        ],
        scratch_shapes=[pltpu.VMEM((1, LANES), f32)],
        compiler_params=pltpu.CompilerParams(
            dimension_semantics=("arbitrary",), vmem_limit_bytes=VMEM_LIMIT_BYTES),
        name="attn_in",
    )(x, g, w_main, w_f, b_f)


def _fox_attn_kernel(qta_ref, ka_ref, vta_ref, ot_ref, m_ref, acc_ref, *, tq, tk):
    i = pl.program_id(1)
    qa = qta_ref[...]
    m_ref[...] = jnp.full_like(m_ref, MASK_VALUE)
    acc_ref[...] = jnp.zeros_like(acc_ref)

    def chunk(start, mask):
        k = ka_ref[pl.ds(start, tk), :]
        s_t = jnp.dot(k, qa, preferred_element_type=f32)
        if mask is not None:
            s_t = jnp.where(mask, s_t, MASK_VALUE)
        m_prev = m_ref[...]
        m_new = jnp.maximum(m_prev, jnp.max(s_t, axis=0, keepdims=True))
        alpha = jnp.exp2(m_prev - m_new)
        p_t = jnp.exp2(s_t - m_new).astype(bf16)
        pv = jnp.dot(vta_ref[:, pl.ds(start, tk)], p_t, preferred_element_type=f32)
        acc_ref[...] = alpha * acc_ref[...] + pv
        m_ref[...] = m_new

    def body(j, carry):
        chunk(pl.multiple_of(j * tk, tk), None)
        return carry
    lax.fori_loop(0, i * (tq // tk), body, 0)

    r = lax.broadcasted_iota(jnp.int32, (tk, tq), 0)
    c = lax.broadcasted_iota(jnp.int32, (tk, tq), 1)
    for dch in range(tq // tk):
        chunk(pl.multiple_of(i * tq + dch * tk, tk), r + dch * tk <= c)

    acc = acc_ref[...]
    inv_l = 1.0 / acc[HEAD_DIM:HEAD_DIM + 1, :]
    ot_ref[...] = (acc[:HEAD_DIM, :] * inv_l).astype(bf16)


def _fox_attn(qta, ka, vta, *, tq, tk):
    s = ka.shape[0]
    kern = functools.partial(_fox_attn_kernel, tq=tq, tk=tk)
    return pl.pallas_call(
        kern,
        grid=(N_HEADS, s // tq),
        in_specs=[
            pl.BlockSpec((2 * HEAD_DIM, tq), lambda h, i: (h, i)),
            pl.BlockSpec((s, LANES), lambda h, i: (0, h)),
            pl.BlockSpec((VT_ROWS, s), lambda h, i: (h, 0)),
        ],
        out_specs=pl.BlockSpec((HEAD_DIM, tq), lambda h, i: (h, i)),
        out_shape=jax.ShapeDtypeStruct((N_HEADS * HEAD_DIM, s), bf16),
        scratch_shapes=[pltpu.VMEM((1, tq), f32), pltpu.VMEM((VT_ROWS, tq), f32)],
        compiler_params=pltpu.CompilerParams(
            dimension_semantics=("arbitrary", "arbitrary"),
            vmem_limit_bytes=VMEM_LIMIT_BYTES),
        name="fox_attn",
    )(qta, ka, vta)


def _conv_in_kernel(x_ref, g_ref, w_ref, cw_ref, a_ref, tail_ref, *, ts, d):
    @pl.when(pl.program_id(0) == 0)
    def _():
        tail_ref[...] = jnp.zeros_like(tail_ref)

    h = _rmsnorm(x_ref[...], g_ref[...]).astype(bf16)
    proj = jnp.dot(h, w_ref[...], preferred_element_type=f32)
    gate_b = proj[:, 0:d]
    u = proj[:, d:2 * d] * proj[:, 2 * d:3 * d]
    z = proj[:, 3 * d:4 * d]

    rows = lax.broadcasted_iota(jnp.int32, (ts, d), 0)
    tail = tail_ref[...]
    prev2 = jnp.broadcast_to(tail[0:1, :], (ts, d))
    prev1 = jnp.broadcast_to(tail[1:2, :], (ts, d))
    u1 = jnp.where(rows == 0, prev1, pltpu.roll(u, 1, axis=0))
    u2 = jnp.where(rows == 0, prev2, jnp.where(rows == 1, prev1, pltpu.roll(u, 2, axis=0)))
    tail_ref[0:2, :] = u[ts - 2:ts, :]
    cw = cw_ref[...]
    y = u2 * cw[0:1, :] + u1 * cw[1:2, :] + u * cw[2:3, :]
    a_ref[...] = (gate_b * y * (z * jax.nn.sigmoid(z))).astype(bf16)


def _conv_in(x, g, w_in, conv_w, *, ts):
    s, d = x.shape
    kern = functools.partial(_conv_in_kernel, ts=ts, d=d)
    return pl.pallas_call(
        kern,
        grid=(s // ts,),
        in_specs=[
            pl.BlockSpec((ts, d), lambda i: (i, 0)),
            _const_spec((1, d)),
            _const_spec((d, 4 * d)),
            _const_spec((8, d)),
        ],
        out_specs=pl.BlockSpec((ts, d), lambda i: (i, 0)),
        out_shape=jax.ShapeDtypeStruct((s, d), bf16),
        scratch_shapes=[pltpu.VMEM((8, d), f32)],
        compiler_params=pltpu.CompilerParams(
            dimension_semantics=("arbitrary",), vmem_limit_bytes=VMEM_LIMIT_BYTES),
        name="conv_in",
    )(x, g, w_in, conv_w)


def _mix_out_kernel(*refs, transposed):
    if transposed:
        (x_ref, ot_ref, gate_ref, wo_ref, gpost_ref, gpe_ref, wg_ref, p_ref, wp_ref, o_ref) = refs
        a = (ot_ref[...].astype(f32).T * gate_ref[...].astype(f32)).astype(bf16)
    else:
        (x_ref, a_ref, wo_ref, gpost_ref, gpe_ref, wg_ref, p_ref, wp_ref, o_ref) = refs
        a = a_ref[...]
    y = jnp.dot(a, wo_ref[...], preferred_element_type=f32)
    x1 = x_ref[...] + _rmsnorm(y, gpost_ref[...])
    hg = _rmsnorm(x1, gpe_ref[...]).astype(bf16)
    gate = jax.nn.sigmoid(jnp.dot(hg, wg_ref[...], preferred_element_type=f32))
    pp = jnp.dot(p_ref[...].astype(bf16), wp_ref[...], preferred_element_type=f32)
    o_ref[...] = x1 + gate * pp


def _mix_out(x, acts, w_out, g_post, g_pe, w_gate, p_i, w_proj, *, ts, transposed):
    s, d = x.shape
    ple = p_i.shape[1]
    row_spec = pl.BlockSpec((ts, d), lambda i: (i, 0))
    if transposed:
        act_specs = [pl.BlockSpec((d, ts), lambda i: (0, i)), row_spec]
    else:
        act_specs = [row_spec]
    kern = functools.partial(_mix_out_kernel, transposed=transposed)
    return pl.pallas_call(
        kern,
        grid=(s // ts,),
        in_specs=[row_spec] + act_specs + [
            _const_spec((d, d)),
            _const_spec((1, d)),
            _const_spec((1, d)),
            _const_spec((d, d)),
            pl.BlockSpec((ts, ple), lambda i: (i, 0)),
            _const_spec((ple, d)),
        ],
        out_specs=row_spec,
        out_shape=jax.ShapeDtypeStruct((s, d), f32),
        compiler_params=pltpu.CompilerParams(
            dimension_semantics=("arbitrary",), vmem_limit_bytes=VMEM_LIMIT_BYTES),
        name="mix_out_t" if transposed else "mix_out",
    )(x, *acts, w_out, g_post, g_pe, w_gate, p_i, w_proj)


def _tile(s, want):
    return math.gcd(s, want)


def kernel(x, p, attn_w_in, attn_b_f, attn_w_out, conv_w_in, conv_kernel, conv_w_out,
           norm_pre, norm_post, pe_norm, pe_w_gate, pe_w_proj):
    b, s, d = x.shape
    assert b == 1 and d == N_HEADS * HEAD_DIM
    depth = p.shape[0]
    ts = _tile(s, 512)
    tq = _tile(s, 512)
    tk = _tile(s, 512)
    xs = x.reshape(s, d)
    for i in range(depth):
        j = i // 2
        g_pre = norm_pre[i].reshape(1, d)
        g_post = norm_post[i].reshape(1, d)
        g_pe = pe_norm[i].reshape(1, d)
        w_gate = pe_w_gate[i].astype(bf16)
        w_proj = pe_w_proj[i].astype(bf16)
        if i % 2 == 0:
            w_in = attn_w_in[j]
            w_main = w_in[:, :4 * d].astype(bf16)
            w_f = jnp.pad(w_in[:, 4 * d:], ((0, 0), (0, LANES - N_HEADS))).astype(bf16)
            b_f = jnp.pad(attn_b_f[j], (0, LANES - N_HEADS)).reshape(1, LANES)
            qta, ka, vta, gate = _attn_in(xs, g_pre, w_main, w_f, b_f, ts=ts)
            ot = _fox_attn(qta, ka, vta, tq=tq, tk=tk)
            xs = _mix_out(xs, (ot, gate), attn_w_out[j].astype(bf16), g_post, g_pe,
                          w_gate, p[i, 0], w_proj, ts=ts, transposed=True)
        else:
            cw = jnp.pad(conv_kernel[j], ((0, 8 - CONV_TAPS), (0, 0)))
            a = _conv_in(xs, g_pre, conv_w_in[j].astype(bf16), cw, ts=ts)
            xs = _mix_out(xs, (a,), conv_w_out[j].astype(bf16), g_post, g_pe,
                          w_gate, p[i, 0], w_proj, ts=ts, transposed=False)
    return xs.reshape(b, s, d)
```

```python
import functools
import math

import jax
import jax.numpy as jnp
from jax import lax
from jax.experimental import pallas as pl
from jax.experimental.pallas import tpu as pltpu

N_HEADS = 16
HEAD_DIM = 64
CONV_TAPS = 3
RMS_EPS = 1e-6
LOG2E = math.log2(math.e)

LANES = 128
AUG_ROWS = 16
VT_ROWS = HEAD_DIM + AUG_ROWS
VMEM_LIMIT_BYTES = 56 * 1024 * 1024
MASK_VALUE = -1e30

f32 = jnp.float32
bf16 = jnp.bfloat16


def _rmsnorm(x, g):
    ms = jnp.mean(x * x, axis=-1, keepdims=True)
    return x * lax.rsqrt(ms + RMS_EPS) * g


def _split3(c):
    hi = c.astype(bf16).astype(f32)
    r = c - hi
    mid = r.astype(bf16).astype(f32)
    lo = (r - mid).astype(bf16).astype(f32)
    return hi, mid, lo


def _log_sigmoid(x):
    return jnp.minimum(x, 0.0) - jnp.log1p(jnp.exp(-jnp.abs(x)))


def _const_spec(shape):
    return pl.BlockSpec(shape, lambda *_: (0,) * len(shape))


def _attn_in_kernel(x_ref, g_ref, w_ref, wf_ref, bf_ref,
                    qta_ref, ka_ref, vta_ref, gate_ref, carry_ref, *, ts, d):
    @pl.when(pl.program_id(0) == 0)
    def _():
        carry_ref[...] = jnp.zeros_like(carry_ref)

    h = _rmsnorm(x_ref[...], g_ref[...]).astype(bf16)
    proj = jnp.dot(h, w_ref[...], preferred_element_type=f32)
    f_logit = jnp.dot(h, wf_ref[...], preferred_element_type=f32) + bf_ref[...]
    logf2 = _log_sigmoid(f_logit) * LOG2E

    rows = lax.broadcasted_iota(jnp.int32, (ts, LANES), 0)
    c = logf2
    shift = 1
    while shift < ts:
        c = c + jnp.where(rows >= shift, pltpu.roll(c, shift, axis=0), 0.0)
        shift *= 2
    c = c + carry_ref[...]
    carry_ref[...] = c[ts - 1:ts, :]
    c_t = c.T

    q_scale = LOG2E / math.sqrt(HEAD_DIM)
    lane = lax.broadcasted_iota(jnp.int32, (ts, LANES), 1)
    row = lax.broadcasted_iota(jnp.int32, (HEAD_DIM, ts), 0)
    vrow = lax.broadcasted_iota(jnp.int32, (AUG_ROWS, ts), 0)
    ones_blk = jnp.where(vrow == 0, 1.0, 0.0).astype(bf16)
    for pair in range(N_HEADS // 2):
        lo_col = pair * LANES
        q_t = (proj[:, lo_col:lo_col + LANES] * q_scale).T
        k_pair = proj[:, d + lo_col:d + lo_col + LANES]
        k_swap = pltpu.roll(k_pair, HEAD_DIM, axis=1)
        v_t = proj[:, 2 * d + lo_col:2 * d + lo_col + LANES].T
        for e in range(2):
            hd = 2 * pair + e
            hi, mid, lo = _split3(jnp.broadcast_to(c[:, hd:hd + 1], (ts, LANES)))
            k_h = k_pair if e == 0 else k_swap
            ka = jnp.where(lane < HEAD_DIM, k_h,
                 jnp.where(lane < HEAD_DIM + 3, 1.0,
                 jnp.where(lane == HEAD_DIM + 3, -hi,
                 jnp.where(lane == HEAD_DIM + 4, -mid,
                 jnp.where(lane == HEAD_DIM + 5, -lo, 0.0)))))
            ka_ref[:, hd * LANES:(hd + 1) * LANES] = ka.astype(bf16)
            hi, mid, lo = _split3(jnp.broadcast_to(c_t[hd:hd + 1, :], (HEAD_DIM, ts)))
            aug = jnp.where(row == 0, hi,
                  jnp.where(row == 1, mid,
                  jnp.where(row == 2, lo,
                  jnp.where(row < 6, 1.0, 0.0))))
            base = hd * 2 * HEAD_DIM
            qta_ref[base:base + HEAD_DIM, :] = q_t[e * HEAD_DIM:(e + 1) * HEAD_DIM].astype(bf16)
            qta_ref[base + HEAD_DIM:base + 2 * HEAD_DIM, :] = aug.astype(bf16)
            vbase = hd * VT_ROWS
            vta_ref[vbase:vbase + HEAD_DIM, :] = v_t[e * HEAD_DIM:(e + 1) * HEAD_DIM].astype(bf16)
            vta_ref[vbase + HEAD_DIM:vbase + VT_ROWS, :] = ones_blk
    z = proj[:, 3 * d:4 * d]
    gate_ref[...] = (z * jax.nn.sigmoid(z)).astype(bf16)


def _attn_in(x, g, w_main, w_f, b_f, *, ts):
    s, d = x.shape
    kern = functools.partial(_attn_in_kernel, ts=ts, d=d)
    return pl.pallas_call(
        kern,
        grid=(s // ts,),
        in_specs=[
            pl.BlockSpec((ts, d), lambda i: (i, 0)),
            _const_spec((1, d)),
            _const_spec((d, 4 * d)),
            _const_spec((d, LANES)),
            _const_spec((1, LANES)),
        ],
        out_specs=[
            pl.BlockSpec((N_HEADS * 2 * HEAD_DIM, ts), lambda i: (0, i)),
            pl.BlockSpec((ts, N_HEADS * LANES), lambda i: (i, 0)),
            pl.BlockSpec((N_HEADS * VT_ROWS, ts), lambda i: (0, i)),
            pl.BlockSpec((ts, d), lambda i: (i, 0)),
        ],
        out_shape=[
            jax.ShapeDtypeStruct((N_HEADS * 2 * HEAD_DIM, s), bf16),
            jax.ShapeDtypeStruct((s, N_HEADS * LANES), bf16),
            jax.ShapeDtypeStruct((N_HEADS * VT_ROWS, s), bf16),
            jax.ShapeDtypeStruct((s, d), bf16),
        ],
        scratch_shapes=[pltpu.VMEM((1, LANES), f32)],
        compiler_params=pltpu.CompilerParams(
            dimension_semantics=("arbitrary",), vmem_limit_bytes=VMEM_LIMIT_BYTES),
        name="attn_in",
    )(x, g, w_main, w_f, b_f)


def _fox_attn_kernel(qta_ref, ka_ref, vta_ref, ot_ref,
                     s0, s1, p0, p1, a0, a1, m_ref, acc_ref, *, tq, tk):
    i = pl.program_id(1)
    nsub = tq // tk
    s_buf, p_buf, alpha_buf = (s0, s1), (p0, p1), (a0, a1)

    def scores(c, slot, col0=0):
        start = pl.multiple_of(c * tk, tk)
        s_buf[slot][:, col0:] = jnp.dot(ka_ref[pl.ds(start, tk), :], qta_ref[:, col0:],
                                        preferred_element_type=f32)

    def softmax(slot, col0=0, diagonal=False):
        for g in range(col0 // LANES, tq // LANES):
            cols = slice(g * LANES, (g + 1) * LANES)
            s_g = s_buf[slot][:, cols]
            if diagonal and g * LANES < col0 + tk:
                r = lax.broadcasted_iota(jnp.int32, (tk, LANES), 0)
                col = lax.broadcasted_iota(jnp.int32, (tk, LANES), 1) + (g * LANES - col0)
                s_g = jnp.where(r <= col, s_g, MASK_VALUE)
            m_prev = m_ref[:, cols]
            m_new = jnp.maximum(m_prev, jnp.max(s_g, axis=0, keepdims=True))
            alpha_buf[slot][:, cols] = jnp.exp2(m_prev - m_new)
            m_ref[:, cols] = m_new
            p_buf[slot][:, cols] = jnp.exp2((s_g - m_new).astype(bf16))

    def accumulate(c, slot, col0=0):
        start = pl.multiple_of(c * tk, tk)
        pv = jnp.dot(vta_ref[:, pl.ds(start, tk)], p_buf[slot][:, col0:],
                     preferred_element_type=f32)
        acc_ref[:, col0:] = alpha_buf[slot][:, col0:] * acc_ref[:, col0:] + pv

    m_ref[...] = jnp.full_like(m_ref, MASK_VALUE)
    acc_ref[...] = jnp.zeros_like(acc_ref)
    p1[...] = jnp.zeros_like(p1)
    a1[...] = jnp.ones_like(a1)
    scores(0, 0)

    def body(ii, carry):
        for u in range(nsub):
            c = ii * nsub + u
            scores(c + 1, (u + 1) % 2)
            softmax(u % 2)
            accumulate(jnp.maximum(c - 1, 0), (u + 1) % 2)
        return carry
    lax.fori_loop(0, i, body, 0)

    first = i * nsub
    for d in range(nsub):
        if d + 1 < nsub:
            scores(first + d + 1, (d + 1) % 2, (d + 1) * tk)
        softmax(d % 2, d * tk, diagonal=True)
        accumulate(jnp.maximum(first + d - 1, 0), (d + 1) % 2, max(d - 1, 0) * tk)
    accumulate(first + nsub - 1, (nsub - 1) % 2, (nsub - 1) * tk)

    acc = acc_ref[...]
    inv_l = 1.0 / acc[HEAD_DIM:HEAD_DIM + 1, :]
    ot_ref[...] = (acc[:HEAD_DIM, :] * inv_l).astype(bf16)


def _fox_attn(qta, ka, vta, *, tq, tk):
    s = ka.shape[0]
    assert (tq // tk) % 2 == 0
    kern = functools.partial(_fox_attn_kernel, tq=tq, tk=tk)
    return pl.pallas_call(
        kern,
        grid=(N_HEADS, s // tq),
        in_specs=[
            pl.BlockSpec((2 * HEAD_DIM, tq), lambda h, i: (h, i)),
            pl.BlockSpec((s, LANES), lambda h, i: (0, h)),
            pl.BlockSpec((VT_ROWS, s), lambda h, i: (h, 0)),
        ],
        out_specs=pl.BlockSpec((HEAD_DIM, tq), lambda h, i: (h, i)),
        out_shape=jax.ShapeDtypeStruct((N_HEADS * HEAD_DIM, s), bf16),
        scratch_shapes=[
            pltpu.VMEM((tk, tq), f32), pltpu.VMEM((tk, tq), f32),
            pltpu.VMEM((tk, tq), bf16), pltpu.VMEM((tk, tq), bf16),
            pltpu.VMEM((1, tq), f32), pltpu.VMEM((1, tq), f32),
            pltpu.VMEM((1, tq), f32),
            pltpu.VMEM((VT_ROWS, tq), f32),
        ],
        compiler_params=pltpu.CompilerParams(
            dimension_semantics=("arbitrary", "arbitrary"),
            vmem_limit_bytes=VMEM_LIMIT_BYTES),
        name="fox_attn",
    )(qta, ka, vta)


def _conv_in_kernel(x_ref, g_ref, w_ref, cw_ref, a_ref, tail_ref, *, ts, d):
    @pl.when(pl.program_id(0) == 0)
    def _():
        tail_ref[...] = jnp.zeros_like(tail_ref)

    h = _rmsnorm(x_ref[...], g_ref[...]).astype(bf16)
    proj = jnp.dot(h, w_ref[...], preferred_element_type=f32)
    gate_b = proj[:, 0:d]
    u = proj[:, d:2 * d] * proj[:, 2 * d:3 * d]
    z = proj[:, 3 * d:4 * d]

    rows = lax.broadcasted_iota(jnp.int32, (ts, d), 0)
    tail = tail_ref[...]
    prev2 = jnp.broadcast_to(tail[0:1, :], (ts, d))
    prev1 = jnp.broadcast_to(tail[1:2, :], (ts, d))
    u1 = jnp.where(rows == 0, prev1, pltpu.roll(u, 1, axis=0))
    u2 = jnp.where(rows == 0, prev2, jnp.where(rows == 1, prev1, pltpu.roll(u, 2, axis=0)))
    tail_ref[0:2, :] = u[ts - 2:ts, :]
    cw = cw_ref[...]
    y = u2 * cw[0:1, :] + u1 * cw[1:2, :] + u * cw[2:3, :]
    a_ref[...] = (gate_b * y * (z * jax.nn.sigmoid(z))).astype(bf16)


def _conv_in(x, g, w_in, conv_w, *, ts):
    s, d = x.shape
    kern = functools.partial(_conv_in_kernel, ts=ts, d=d)
    return pl.pallas_call(
        kern,
        grid=(s // ts,),
        in_specs=[
            pl.BlockSpec((ts, d), lambda i: (i, 0)),
            _const_spec((1, d)),
            _const_spec((d, 4 * d)),
            _const_spec((8, d)),
        ],
        out_specs=pl.BlockSpec((ts, d), lambda i: (i, 0)),
        out_shape=jax.ShapeDtypeStruct((s, d), bf16),
        scratch_shapes=[pltpu.VMEM((8, d), f32)],
        compiler_params=pltpu.CompilerParams(
            dimension_semantics=("arbitrary",), vmem_limit_bytes=VMEM_LIMIT_BYTES),
        name="conv_in",
    )(x, g, w_in, conv_w)


def _mix_out_kernel(*refs, transposed):
    if transposed:
        (x_ref, ot_ref, gate_ref, wo_ref, gpost_ref, gpe_ref, wg_ref, p_ref, wp_ref, o_ref) = refs
        a = (ot_ref[...].astype(f32).T * gate_ref[...].astype(f32)).astype(bf16)
    else:
        (x_ref, a_ref, wo_ref, gpost_ref, gpe_ref, wg_ref, p_ref, wp_ref, o_ref) = refs
        a = a_ref[...]
    y = jnp.dot(a, wo_ref[...], preferred_element_type=f32)
    x1 = x_ref[...] + _rmsnorm(y, gpost_ref[...])
    hg = _rmsnorm(x1, gpe_ref[...]).astype(bf16)
    gate = jax.nn.sigmoid(jnp.dot(hg, wg_ref[...], preferred_element_type=f32))
    pp = jnp.dot(p_ref[...].astype(bf16), wp_ref[...], preferred_element_type=f32)
    o_ref[...] = x1 + gate * pp


def _mix_out(x, acts, w_out, g_post, g_pe, w_gate, p_i, w_proj, *, ts, transposed):
    s, d = x.shape
    ple = p_i.shape[1]
    row_spec = pl.BlockSpec((ts, d), lambda i: (i, 0))
    if transposed:
        act_specs = [pl.BlockSpec((d, ts), lambda i: (0, i)), row_spec]
    else:
        act_specs = [row_spec]
    kern = functools.partial(_mix_out_kernel, transposed=transposed)
    return pl.pallas_call(
        kern,
        grid=(s // ts,),
        in_specs=[row_spec] + act_specs + [
            _const_spec((d, d)),
            _const_spec((1, d)),
            _const_spec((1, d)),
            _const_spec((d, d)),
            pl.BlockSpec((ts, ple), lambda i: (i, 0)),
            _const_spec((ple, d)),
        ],
        out_specs=row_spec,
        out_shape=jax.ShapeDtypeStruct((s, d), f32),
        compiler_params=pltpu.CompilerParams(
            dimension_semantics=("arbitrary",), vmem_limit_bytes=VMEM_LIMIT_BYTES),
        name="mix_out_t" if transposed else "mix_out",
    )(x, *acts, w_out, g_post, g_pe, w_gate, p_i, w_proj)


def _tile(s, want):
    return math.gcd(s, want)


def kernel(x, p, attn_w_in, attn_b_f, attn_w_out, conv_w_in, conv_kernel, conv_w_out,
           norm_pre, norm_post, pe_norm, pe_w_gate, pe_w_proj):
    b, s, d = x.shape
    assert b == 1 and d == N_HEADS * HEAD_DIM
    depth = p.shape[0]
    ts = _tile(s, 512)
    tq = _tile(s, 1024)
    tk = _tile(s, 256)
    xs = x.reshape(s, d)
    for i in range(depth):
        j = i // 2
        g_pre = norm_pre[i].reshape(1, d)
        g_post = norm_post[i].reshape(1, d)
        g_pe = pe_norm[i].reshape(1, d)
        w_gate = pe_w_gate[i].astype(bf16)
        w_proj = pe_w_proj[i].astype(bf16)
        if i % 2 == 0:
            w_in = attn_w_in[j]
            w_main = w_in[:, :4 * d].astype(bf16)
            w_f = jnp.pad(w_in[:, 4 * d:], ((0, 0), (0, LANES - N_HEADS))).astype(bf16)
            b_f = jnp.pad(attn_b_f[j], (0, LANES - N_HEADS)).reshape(1, LANES)
            qta, ka, vta, gate = _attn_in(xs, g_pre, w_main, w_f, b_f, ts=ts)
            ot = _fox_attn(qta, ka, vta, tq=tq, tk=tk)
            xs = _mix_out(xs, (ot, gate), attn_w_out[j].astype(bf16), g_post, g_pe,
                          w_gate, p[i, 0], w_proj, ts=ts, transposed=True)
        else:
            cw = jnp.pad(conv_kernel[j], ((0, 8 - CONV_TAPS), (0, 0)))
            a = _conv_in(xs, g_pre, conv_w_in[j].astype(bf16), cw, ts=ts)
            xs = _mix_out(xs, (a,), conv_w_out[j].astype(bf16), g_post, g_pe,
                          w_gate, p[i, 0], w_proj, ts=ts, transposed=False)
    return xs.reshape(b, s, d)
```

```python
import functools
import math

import jax
import jax.numpy as jnp
from jax import lax
from jax.experimental import pallas as pl
from jax.experimental.pallas import tpu as pltpu

N_HEADS = 16
HEAD_DIM = 64
CONV_TAPS = 3
RMS_EPS = 1e-6
LOG2E = math.log2(math.e)

LANES = 128
SUBLANES = 8
BF16_ROWS = 2 * SUBLANES
AUG_ROWS = 16
VT_ROWS = HEAD_DIM + AUG_ROWS
VMEM_LIMIT_BYTES = 56 * 1024 * 1024
MASK_VALUE = -(2.0 ** 100)

f32 = jnp.float32
bf16 = jnp.bfloat16


def _rmsnorm(x, g):
    ms = jnp.mean(x * x, axis=-1, keepdims=True)
    return x * lax.rsqrt(ms + RMS_EPS) * g


def _split3(c):
    hi = c.astype(bf16).astype(f32)
    r = c - hi
    mid = r.astype(bf16).astype(f32)
    lo = (r - mid).astype(bf16).astype(f32)
    return hi, mid, lo


def _log_sigmoid(x):
    return jnp.minimum(x, 0.0) - jnp.log1p(jnp.exp(-jnp.abs(x)))


def _const_spec(shape):
    return pl.BlockSpec(shape, lambda *_: (0,) * len(shape))


def _attn_in_kernel(x_ref, g_ref, w_ref, wf_ref, bf_ref,
                    qta_ref, ka_ref, vta_ref, gate_ref, carry_ref, *, ts, d):
    @pl.when(pl.program_id(0) == 0)
    def _():
        carry_ref[...] = jnp.zeros_like(carry_ref)

    h = _rmsnorm(x_ref[...], g_ref[...]).astype(bf16)
    proj = jnp.dot(h, w_ref[...], preferred_element_type=f32)
    f_logit = jnp.dot(h, wf_ref[...], preferred_element_type=f32) + bf_ref[...]
    logf2 = _log_sigmoid(f_logit) * LOG2E

    rows = lax.broadcasted_iota(jnp.int32, (ts, LANES), 0)
    c = logf2
    shift = 1
    while shift < ts:
        c = c + jnp.where(rows >= shift, pltpu.roll(c, shift, axis=0), 0.0)
        shift *= 2
    c = c + carry_ref[...]
    carry_ref[...] = c[ts - 1:ts, :]
    c_t = c.T

    q_scale = LOG2E / math.sqrt(HEAD_DIM)
    lane = lax.broadcasted_iota(jnp.int32, (ts, LANES), 1)
    row = lax.broadcasted_iota(jnp.int32, (HEAD_DIM, ts), 0)
    vrow = lax.broadcasted_iota(jnp.int32, (AUG_ROWS, ts), 0)
    ones_blk = jnp.where(vrow == 0, 1.0, 0.0).astype(bf16)
    for pair in range(N_HEADS // 2):
        lo_col = pair * LANES
        q_t = (proj[:, lo_col:lo_col + LANES] * q_scale).T
        k_pair = proj[:, d + lo_col:d + lo_col + LANES]
        k_swap = pltpu.roll(k_pair, HEAD_DIM, axis=1)
        v_t = proj[:, 2 * d + lo_col:2 * d + lo_col + LANES].T
        for e in range(2):
            hd = 2 * pair + e
            hi, mid, lo = _split3(jnp.broadcast_to(c[:, hd:hd + 1], (ts, LANES)))
            k_h = k_pair if e == 0 else k_swap
            ka = jnp.where(lane < HEAD_DIM, k_h,
                 jnp.where(lane < HEAD_DIM + 3, 1.0,
                 jnp.where(lane == HEAD_DIM + 3, -hi,
                 jnp.where(lane == HEAD_DIM + 4, -mid,
                 jnp.where(lane == HEAD_DIM + 5, -lo, 0.0)))))
            ka_ref[:, hd * LANES:(hd + 1) * LANES] = ka.astype(bf16)
            hi, mid, lo = _split3(jnp.broadcast_to(c_t[hd:hd + 1, :], (HEAD_DIM, ts)))
            aug = jnp.where(row == 0, hi,
                  jnp.where(row == 1, mid,
                  jnp.where(row == 2, lo,
                  jnp.where(row < 6, 1.0, 0.0))))
            base = hd * 2 * HEAD_DIM
            qta_ref[base:base + HEAD_DIM, :] = q_t[e * HEAD_DIM:(e + 1) * HEAD_DIM].astype(bf16)
            qta_ref[base + HEAD_DIM:base + 2 * HEAD_DIM, :] = aug.astype(bf16)
            vbase = hd * VT_ROWS
            vta_ref[vbase:vbase + HEAD_DIM, :] = v_t[e * HEAD_DIM:(e + 1) * HEAD_DIM].astype(bf16)
            vta_ref[vbase + HEAD_DIM:vbase + VT_ROWS, :] = ones_blk
    z = proj[:, 3 * d:4 * d]
    gate_ref[...] = (z * jax.nn.sigmoid(z)).astype(bf16)


def _attn_in(x, g, w_main, w_f, b_f, *, ts):
    s, d = x.shape
    kern = functools.partial(_attn_in_kernel, ts=ts, d=d)
    return pl.pallas_call(
        kern,
        grid=(s // ts,),
        in_specs=[
            pl.BlockSpec((ts, d), lambda i: (i, 0)),
            _const_spec((1, d)),
            _const_spec((d, 4 * d)),
            _const_spec((d, LANES)),
            _const_spec((1, LANES)),
        ],
        out_specs=[
            pl.BlockSpec((N_HEADS * 2 * HEAD_DIM, ts), lambda i: (0, i)),
            pl.BlockSpec((ts, N_HEADS * LANES), lambda i: (i, 0)),
            pl.BlockSpec((N_HEADS * VT_ROWS, ts), lambda i: (0, i)),
            pl.BlockSpec((ts, d), lambda i: (i, 0)),
        ],
        out_shape=[
            jax.ShapeDtypeStruct((N_HEADS * 2 * HEAD_DIM, s), bf16),
            jax.ShapeDtypeStruct((s, N_HEADS * LANES), bf16),
            jax.ShapeDtypeStruct((N_HEADS * VT_ROWS, s), bf16),
            jax.ShapeDtypeStruct((s, d), bf16),
        ],
        scratch_shapes=[pltpu.VMEM((1, LANES), f32)],
        compiler_params=pltpu.CompilerParams(
            dimension_semantics=("arbitrary",), vmem_limit_bytes=VMEM_LIMIT_BYTES),
        name="attn_in",
    )(x, g, w_main, w_f, b_f)


def _fox_attn_kernel(qta_ref, ka_ref, vta_ref, ot_ref, *scratch, tq, tk):
    i = pl.program_id(1)
    nsub = tq // tk
    s_buf, p_buf, alpha_buf = (scratch[k * nsub:(k + 1) * nsub] for k in range(3))
    m_ref, acc_ref = scratch[3 * nsub:]

    def scores(c, slot, col0=0):
        start = pl.multiple_of(c * tk, tk)
        s_t = jnp.dot(ka_ref[pl.ds(start, tk), :], qta_ref[:, col0:], preferred_element_type=f32)
        s_buf[slot][:, col0:] = s_t.astype(bf16)

    def softmax(slot, col0=0, diagonal=False):
        for g in range(col0 // LANES, tq // LANES):
            cols = slice(g * LANES, (g + 1) * LANES)
            s_g = s_buf[slot][:, cols]
            if diagonal and g * LANES < col0 + tk:
                r = lax.broadcasted_iota(jnp.int32, (tk, LANES), 0)
                col = lax.broadcasted_iota(jnp.int32, (tk, LANES), 1) + (g * LANES - col0)
                s_g = jnp.where(r <= col, s_g, MASK_VALUE)
            s_g = s_g.reshape(tk // BF16_ROWS, BF16_ROWS, LANES)
            m16 = jnp.max(s_g, axis=0).astype(f32)
            m8 = jnp.maximum(m16[:SUBLANES], m16[SUBLANES:])
            for sh in (4, 2, 1):
                m8 = jnp.maximum(m8, pltpu.roll(m8, sh, axis=0))
            m_prev = m_ref[:, cols]
            m_new = jnp.maximum(m_prev, m8)
            alpha_buf[slot][:, cols] = jnp.exp2(m_prev - m_new)
            m_ref[:, cols] = m_new
            m_b = jnp.concatenate([m_new, m_new], axis=0).astype(bf16)
            p_buf[slot][:, cols] = jnp.exp2(s_g - m_b[None]).reshape(tk, LANES)

    def accumulate(c, slot, col0=0):
        start = pl.multiple_of(c * tk, tk)
        w = tq - col0
        pv = jnp.dot(vta_ref[:, pl.ds(start, tk)], p_buf[slot][:, col0:],
                     preferred_element_type=f32)
        acc3 = acc_ref[:, col0:].reshape(VT_ROWS // SUBLANES, SUBLANES, w)
        acc3 = alpha_buf[slot][:, col0:][None] * acc3 + pv.reshape(acc3.shape)
        acc_ref[:, col0:] = acc3.reshape(VT_ROWS, w)

    m_ref[...] = jnp.full_like(m_ref, MASK_VALUE)
    acc_ref[...] = jnp.zeros_like(acc_ref)
    p_buf[nsub - 1][...] = jnp.zeros_like(p_buf[nsub - 1])
    alpha_buf[nsub - 1][...] = jnp.ones_like(alpha_buf[nsub - 1])
    scores(0, 0)

    def body(ii, carry):
        for u in range(nsub):
            c = ii * nsub + u
            softmax(u)
            accumulate(jnp.maximum(c - 1, 0), (u - 1) % nsub)
            scores(c + 1, (u + 1) % nsub)
        return carry
    lax.fori_loop(0, i, body, 0)

    first = i * nsub
    for d in range(nsub):
        if d + 1 < nsub:
            scores(first + d + 1, d + 1, (d + 1) * tk)
        softmax(d, d * tk, diagonal=True)
        accumulate(jnp.maximum(first + d - 1, 0), (d - 1) % nsub, max(d - 1, 0) * tk)
    accumulate(first + nsub - 1, nsub - 1, (nsub - 1) * tk)

    acc = acc_ref[...]
    inv_l = 1.0 / acc[HEAD_DIM:HEAD_DIM + 1, :]
    ot_ref[...] = (acc[:HEAD_DIM, :] * inv_l).astype(bf16)


def _fox_attn(qta, ka, vta, *, tq, tk):
    s = ka.shape[0]
    nsub = tq // tk
    assert nsub >= 2
    kern = functools.partial(_fox_attn_kernel, tq=tq, tk=tk)
    return pl.pallas_call(
        kern,
        grid=(N_HEADS, s // tq),
        in_specs=[
            pl.BlockSpec((2 * HEAD_DIM, tq), lambda h, i: (h, i)),
            pl.BlockSpec((s, LANES), lambda h, i: (0, h)),
            pl.BlockSpec((VT_ROWS, s), lambda h, i: (h, 0)),
        ],
        out_specs=pl.BlockSpec((HEAD_DIM, tq), lambda h, i: (h, i)),
        out_shape=jax.ShapeDtypeStruct((N_HEADS * HEAD_DIM, s), bf16),
        scratch_shapes=(
            [pltpu.VMEM((tk, tq), bf16)] * nsub
            + [pltpu.VMEM((tk, tq), bf16)] * nsub
            + [pltpu.VMEM((SUBLANES, tq), f32)] * nsub
            + [pltpu.VMEM((SUBLANES, tq), f32),
               pltpu.VMEM((VT_ROWS, tq), f32)]
        ),
        compiler_params=pltpu.CompilerParams(
            dimension_semantics=("arbitrary", "arbitrary"),
            vmem_limit_bytes=VMEM_LIMIT_BYTES),
        name="fox_attn",
    )(qta, ka, vta)


def _conv_in_kernel(x_ref, g_ref, w_ref, cw_ref, a_ref, tail_ref, *, ts, d):
    @pl.when(pl.program_id(0) == 0)
    def _():
        tail_ref[...] = jnp.zeros_like(tail_ref)

    h = _rmsnorm(x_ref[...], g_ref[...]).astype(bf16)
    proj = jnp.dot(h, w_ref[...], preferred_element_type=f32)
    gate_b = proj[:, 0:d]
    u = proj[:, d:2 * d] * proj[:, 2 * d:3 * d]
    z = proj[:, 3 * d:4 * d]

    rows = lax.broadcasted_iota(jnp.int32, (ts, d), 0)
    tail = tail_ref[...]
    prev2 = jnp.broadcast_to(tail[0:1, :], (ts, d))
    prev1 = jnp.broadcast_to(tail[1:2, :], (ts, d))
    u1 = jnp.where(rows == 0, prev1, pltpu.roll(u, 1, axis=0))
    u2 = jnp.where(rows == 0, prev2, jnp.where(rows == 1, prev1, pltpu.roll(u, 2, axis=0)))
    tail_ref[0:2, :] = u[ts - 2:ts, :]
    cw = cw_ref[...]
    y = u2 * cw[0:1, :] + u1 * cw[1:2, :] + u * cw[2:3, :]
    a_ref[...] = (gate_b * y * (z * jax.nn.sigmoid(z))).astype(bf16)


def _conv_in(x, g, w_in, conv_w, *, ts):
    s, d = x.shape
    kern = functools.partial(_conv_in_kernel, ts=ts, d=d)
    return pl.pallas_call(
        kern,
        grid=(s // ts,),
        in_specs=[
            pl.BlockSpec((ts, d), lambda i: (i, 0)),
            _const_spec((1, d)),
            _const_spec((d, 4 * d)),
            _const_spec((8, d)),
        ],
        out_specs=pl.BlockSpec((ts, d), lambda i: (i, 0)),
        out_shape=jax.ShapeDtypeStruct((s, d), bf16),
        scratch_shapes=[pltpu.VMEM((8, d), f32)],
        compiler_params=pltpu.CompilerParams(
            dimension_semantics=("arbitrary",), vmem_limit_bytes=VMEM_LIMIT_BYTES),
        name="conv_in",
    )(x, g, w_in, conv_w)


def _mix_out_kernel(*refs, transposed):
    if transposed:
        (x_ref, ot_ref, gate_ref, wo_ref, gpost_ref, gpe_ref, wg_ref, p_ref, wp_ref, o_ref) = refs
        a = (ot_ref[...].astype(f32).T * gate_ref[...].astype(f32)).astype(bf16)
    else:
        (x_ref, a_ref, wo_ref, gpost_ref, gpe_ref, wg_ref, p_ref, wp_ref, o_ref) = refs
        a = a_ref[...]
    y = jnp.dot(a, wo_ref[...], preferred_element_type=f32)
    x1 = x_ref[...] + _rmsnorm(y, gpost_ref[...])
    hg = _rmsnorm(x1, gpe_ref[...]).astype(bf16)
    gate = jax.nn.sigmoid(jnp.dot(hg, wg_ref[...], preferred_element_type=f32))
    pp = jnp.dot(p_ref[...].astype(bf16), wp_ref[...], preferred_element_type=f32)
    o_ref[...] = x1 + gate * pp


def _mix_out(x, acts, w_out, g_post, g_pe, w_gate, p_i, w_proj, *, ts, transposed):
    s, d = x.shape
    ple = p_i.shape[1]
    row_spec = pl.BlockSpec((ts, d), lambda i: (i, 0))
    if transposed:
        act_specs = [pl.BlockSpec((d, ts), lambda i: (0, i)), row_spec]
    else:
        act_specs = [row_spec]
    kern = functools.partial(_mix_out_kernel, transposed=transposed)
    return pl.pallas_call(
        kern,
        grid=(s // ts,),
        in_specs=[row_spec] + act_specs + [
            _const_spec((d, d)),
            _const_spec((1, d)),
            _const_spec((1, d)),
            _const_spec((d, d)),
            pl.BlockSpec((ts, ple), lambda i: (i, 0)),
            _const_spec((ple, d)),
        ],
        out_specs=row_spec,
        out_shape=jax.ShapeDtypeStruct((s, d), f32),
        compiler_params=pltpu.CompilerParams(
            dimension_semantics=("arbitrary",), vmem_limit_bytes=VMEM_LIMIT_BYTES),
        name="mix_out_t" if transposed else "mix_out",
    )(x, *acts, w_out, g_post, g_pe, w_gate, p_i, w_proj)


def _tile(s, want):
    return math.gcd(s, want)


def kernel(x, p, attn_w_in, attn_b_f, attn_w_out, conv_w_in, conv_kernel, conv_w_out,
           norm_pre, norm_post, pe_norm, pe_w_gate, pe_w_proj):
    b, s, d = x.shape
    assert b == 1 and d == N_HEADS * HEAD_DIM
    depth = p.shape[0]
    ts = _tile(s, 512)
    tq = _tile(s, 1024)
    tk = _tile(s, 256)
    xs = x.reshape(s, d)
    for i in range(depth):
        j = i // 2
        g_pre = norm_pre[i].reshape(1, d)
        g_post = norm_post[i].reshape(1, d)
        g_pe = pe_norm[i].reshape(1, d)
        w_gate = pe_w_gate[i].astype(bf16)
        w_proj = pe_w_proj[i].astype(bf16)
        if i % 2 == 0:
            w_in = attn_w_in[j]
            w_main = w_in[:, :4 * d].astype(bf16)
            w_f = jnp.pad(w_in[:, 4 * d:], ((0, 0), (0, LANES - N_HEADS))).astype(bf16)
            b_f = jnp.pad(attn_b_f[j], (0, LANES - N_HEADS)).reshape(1, LANES)
            qta, ka, vta, gate = _attn_in(xs, g_pre, w_main, w_f, b_f, ts=ts)
            ot = _fox_attn(qta, ka, vta, tq=tq, tk=tk)
            xs = _mix_out(xs, (ot, gate), attn_w_out[j].astype(bf16), g_post, g_pe,
                          w_gate, p[i, 0], w_proj, ts=ts, transposed=True)
        else:
            cw = jnp.pad(conv_kernel[j], ((0, 8 - CONV_TAPS), (0, 0)))
            a = _conv_in(xs, g_pre, conv_w_in[j].astype(bf16), cw, ts=ts)
            xs = _mix_out(xs, (a,), conv_w_out[j].astype(bf16), g_post, g_pe,
                          w_gate, p[i, 0], w_proj, ts=ts, transposed=False)
    return xs.reshape(b, s, d)
```

```python
import functools
import math

import jax
import jax.numpy as jnp
from jax import lax
from jax.experimental import pallas as pl
from jax.experimental.pallas import tpu as pltpu

N_HEADS = 16
HEAD_DIM = 64
CONV_TAPS = 3
RMS_EPS = 1e-6
LOG2E = math.log2(math.e)

LANES = 128
SUBLANES = 8
BF16_ROWS = 2 * SUBLANES
AUG_ROWS = 16
VT_ROWS = HEAD_DIM + AUG_ROWS
VMEM_LIMIT_BYTES = 56 * 1024 * 1024
MASK_VALUE = -(2.0 ** 100)

f32 = jnp.float32
bf16 = jnp.bfloat16


def _rmsnorm(x, g):
    ms = jnp.mean(x * x, axis=-1, keepdims=True)
    return x * lax.rsqrt(ms + RMS_EPS) * g


def _split3(c):
    hi = c.astype(bf16).astype(f32)
    r = c - hi
    mid = r.astype(bf16).astype(f32)
    lo = (r - mid).astype(bf16).astype(f32)
    return hi, mid, lo


def _log_sigmoid(x):
    return jnp.minimum(x, 0.0) - jnp.log1p(jnp.exp(-jnp.abs(x)))


def _const_spec(shape):
    return pl.BlockSpec(shape, lambda *_: (0,) * len(shape))


def _attn_in_kernel(x_ref, g_ref, w_ref, wf_ref, bf_ref,
                    qta_ref, ka_ref, vta_ref, gate_ref, carry_ref, *, ts, d):
    @pl.when(pl.program_id(0) == 0)
    def _():
        carry_ref[...] = jnp.zeros_like(carry_ref)

    h = _rmsnorm(x_ref[...], g_ref[...]).astype(bf16)
    proj = jnp.dot(h, w_ref[...], preferred_element_type=f32)
    f_logit = jnp.dot(h, wf_ref[...], preferred_element_type=f32) + bf_ref[...]
    logf2 = _log_sigmoid(f_logit) * LOG2E

    rows = lax.broadcasted_iota(jnp.int32, (ts, LANES), 0)
    c = logf2
    shift = 1
    while shift < ts:
        c = c + jnp.where(rows >= shift, pltpu.roll(c, shift, axis=0), 0.0)
        shift *= 2
    c = c + carry_ref[...]
    carry_ref[...] = c[ts - 1:ts, :]
    c_t = c.T

    q_scale = LOG2E / math.sqrt(HEAD_DIM)
    lane = lax.broadcasted_iota(jnp.int32, (ts, LANES), 1)
    row = lax.broadcasted_iota(jnp.int32, (HEAD_DIM, ts), 0)
    vrow = lax.broadcasted_iota(jnp.int32, (AUG_ROWS, ts), 0)
    ones_blk = jnp.where(vrow == 0, 1.0, 0.0).astype(bf16)
    for pair in range(N_HEADS // 2):
        lo_col = pair * LANES
        q_t = (proj[:, lo_col:lo_col + LANES] * q_scale).T
        k_pair = proj[:, d + lo_col:d + lo_col + LANES]
        k_swap = pltpu.roll(k_pair, HEAD_DIM, axis=1)
        v_t = proj[:, 2 * d + lo_col:2 * d + lo_col + LANES].T
        for e in range(2):
            hd = 2 * pair + e
            hi, mid, lo = _split3(jnp.broadcast_to(c[:, hd:hd + 1], (ts, LANES)))
            k_h = k_pair if e == 0 else k_swap
            ka = jnp.where(lane < HEAD_DIM, k_h,
                 jnp.where(lane < HEAD_DIM + 3, 1.0,
                 jnp.where(lane == HEAD_DIM + 3, -hi,
                 jnp.where(lane == HEAD_DIM + 4, -mid,
                 jnp.where(lane == HEAD_DIM + 5, -lo, 0.0)))))
            ka_ref[:, hd * LANES:(hd + 1) * LANES] = ka.astype(bf16)
            hi, mid, lo = _split3(jnp.broadcast_to(c_t[hd:hd + 1, :], (HEAD_DIM, ts)))
            aug = jnp.where(row == 0, hi,
                  jnp.where(row == 1, mid,
                  jnp.where(row == 2, lo,
                  jnp.where(row < 6, 1.0, 0.0))))
            base = hd * 2 * HEAD_DIM
            qta_ref[base:base + HEAD_DIM, :] = q_t[e * HEAD_DIM:(e + 1) * HEAD_DIM].astype(bf16)
            qta_ref[base + HEAD_DIM:base + 2 * HEAD_DIM, :] = aug.astype(bf16)
            vbase = hd * VT_ROWS
            vta_ref[vbase:vbase + HEAD_DIM, :] = v_t[e * HEAD_DIM:(e + 1) * HEAD_DIM].astype(bf16)
            vta_ref[vbase + HEAD_DIM:vbase + VT_ROWS, :] = ones_blk
    z = proj[:, 3 * d:4 * d]
    gate_ref[...] = (z * jax.nn.sigmoid(z)).astype(bf16)


def _attn_in(x, g, w_main, w_f, b_f, *, ts):
    s, d = x.shape
    kern = functools.partial(_attn_in_kernel, ts=ts, d=d)
    return pl.pallas_call(
        kern,
        grid=(s // ts,),
        in_specs=[
            pl.BlockSpec((ts, d), lambda i: (i, 0)),
            _const_spec((1, d)),
            _const_spec((d, 4 * d)),
            _const_spec((d, LANES)),
            _const_spec((1, LANES)),
        ],
        out_specs=[
            pl.BlockSpec((N_HEADS * 2 * HEAD_DIM, ts), lambda i: (0, i)),
            pl.BlockSpec((ts, N_HEADS * LANES), lambda i: (i, 0)),
            pl.BlockSpec((N_HEADS * VT_ROWS, ts), lambda i: (0, i)),
            pl.BlockSpec((ts, d), lambda i: (i, 0)),
        ],
        out_shape=[
            jax.ShapeDtypeStruct((N_HEADS * 2 * HEAD_DIM, s), bf16),
            jax.ShapeDtypeStruct((s, N_HEADS * LANES), bf16),
            jax.ShapeDtypeStruct((N_HEADS * VT_ROWS, s), bf16),
            jax.ShapeDtypeStruct((s, d), bf16),
        ],
        scratch_shapes=[pltpu.VMEM((1, LANES), f32)],
        compiler_params=pltpu.CompilerParams(
            dimension_semantics=("arbitrary",), vmem_limit_bytes=VMEM_LIMIT_BYTES),
        name="attn_in",
    )(x, g, w_main, w_f, b_f)


def _fox_attn_kernel(qta_ref, ka_ref, vta_ref, ot_ref, *scratch, tq, tk):
    i = pl.program_id(1)
    nsub = tq // tk
    s_buf, p_buf, alpha_buf = (scratch[k * nsub:(k + 1) * nsub] for k in range(3))
    m_ref, acc_ref = scratch[3 * nsub:]

    def scores(c, slot, col0=0):
        start = pl.multiple_of(c * tk, tk)
        s_buf[slot][:, col0:] = jnp.dot(ka_ref[pl.ds(start, tk), :], qta_ref[:, col0:],
                                        preferred_element_type=f32)

    def softmax(slot, col0=0, diagonal=False):
        for g in range(col0 // LANES, tq // LANES):
            cols = slice(g * LANES, (g + 1) * LANES)
            s_g = s_buf[slot][:, cols]
            if diagonal and g * LANES < col0 + tk:
                r = lax.broadcasted_iota(jnp.int32, (tk, LANES), 0)
                col = lax.broadcasted_iota(jnp.int32, (tk, LANES), 1) + (g * LANES - col0)
                s_g = jnp.where(r <= col, s_g, MASK_VALUE)
            s_g = s_g.reshape(tk // SUBLANES, SUBLANES, LANES)
            m8 = jnp.max(s_g, axis=0)
            for sh in (4, 2, 1):
                m8 = jnp.maximum(m8, pltpu.roll(m8, sh, axis=0))
            m_prev = m_ref[:, cols]
            m_new = jnp.maximum(m_prev, m8)
            alpha_buf[slot][:, cols] = jnp.exp2(m_prev - m_new)
            m_ref[:, cols] = m_new
            p = jnp.exp2(s_g - m_new[None]).reshape(tk, LANES)
            p_buf[slot][:, cols] = p.astype(bf16)

    def accumulate(c, slot, col0=0):
        start = pl.multiple_of(c * tk, tk)
        w = tq - col0
        pv = jnp.dot(vta_ref[:, pl.ds(start, tk)], p_buf[slot][:, col0:],
                     preferred_element_type=f32)
        acc3 = acc_ref[:, col0:].reshape(VT_ROWS // SUBLANES, SUBLANES, w)
        acc3 = alpha_buf[slot][:, col0:][None] * acc3 + pv.reshape(acc3.shape)
        acc_ref[:, col0:] = acc3.reshape(VT_ROWS, w)

    m_ref[...] = jnp.full_like(m_ref, MASK_VALUE)
    acc_ref[...] = jnp.zeros_like(acc_ref)
    p_buf[nsub - 1][...] = jnp.zeros_like(p_buf[nsub - 1])
    alpha_buf[nsub - 1][...] = jnp.ones_like(alpha_buf[nsub - 1])
    scores(0, 0)

    def body(ii, carry):
        for u in range(nsub):
            c = ii * nsub + u
            softmax(u)
            accumulate(jnp.maximum(c - 1, 0), (u - 1) % nsub)
            scores(c + 1, (u + 1) % nsub)
        return carry
    lax.fori_loop(0, i, body, 0)

    first = i * nsub
    for d in range(nsub):
        if d + 1 < nsub:
            scores(first + d + 1, d + 1, (d + 1) * tk)
        softmax(d, d * tk, diagonal=True)
        accumulate(jnp.maximum(first + d - 1, 0), (d - 1) % nsub, max(d - 1, 0) * tk)
    accumulate(first + nsub - 1, nsub - 1, (nsub - 1) * tk)

    acc = acc_ref[...]
    inv_l = 1.0 / acc[HEAD_DIM:HEAD_DIM + 1, :]
    ot_ref[...] = (acc[:HEAD_DIM, :] * inv_l).astype(bf16)


def _fox_attn(qta, ka, vta, *, tq, tk):
    s = ka.shape[0]
    nsub = tq // tk
    assert nsub >= 2
    kern = functools.partial(_fox_attn_kernel, tq=tq, tk=tk)
    return pl.pallas_call(
        kern,
        grid=(N_HEADS, s // tq),
        in_specs=[
            pl.BlockSpec((2 * HEAD_DIM, tq), lambda h, i: (h, i)),
            pl.BlockSpec((s, LANES), lambda h, i: (0, h)),
            pl.BlockSpec((VT_ROWS, s), lambda h, i: (h, 0)),
        ],
        out_specs=pl.BlockSpec((HEAD_DIM, tq), lambda h, i: (h, i)),
        out_shape=jax.ShapeDtypeStruct((N_HEADS * HEAD_DIM, s), bf16),
        scratch_shapes=(
            [pltpu.VMEM((tk, tq), f32)] * nsub
            + [pltpu.VMEM((tk, tq), bf16)] * nsub
            + [pltpu.VMEM((SUBLANES, tq), f32)] * nsub
            + [pltpu.VMEM((SUBLANES, tq), f32),
               pltpu.VMEM((VT_ROWS, tq), f32)]
        ),
        compiler_params=pltpu.CompilerParams(
            dimension_semantics=("arbitrary", "arbitrary"),
            vmem_limit_bytes=VMEM_LIMIT_BYTES),
        name="fox_attn",
    )(qta, ka, vta)


def _conv_in_kernel(x_ref, g_ref, w_ref, cw_ref, a_ref, tail_ref, *, ts, d):
    @pl.when(pl.program_id(0) == 0)
    def _():
        tail_ref[...] = jnp.zeros_like(tail_ref)

    h = _rmsnorm(x_ref[...], g_ref[...]).astype(bf16)
    proj = jnp.dot(h, w_ref[...], preferred_element_type=f32)
    gate_b = proj[:, 0:d]
    u = proj[:, d:2 * d] * proj[:, 2 * d:3 * d]
    z = proj[:, 3 * d:4 * d]

    rows = lax.broadcasted_iota(jnp.int32, (ts, d), 0)
    tail = tail_ref[...]
    prev2 = jnp.broadcast_to(tail[0:1, :], (ts, d))
    prev1 = jnp.broadcast_to(tail[1:2, :], (ts, d))
    u1 = jnp.where(rows == 0, prev1, pltpu.roll(u, 1, axis=0))
    u2 = jnp.where(rows == 0, prev2, jnp.where(rows == 1, prev1, pltpu.roll(u, 2, axis=0)))
    tail_ref[0:2, :] = u[ts - 2:ts, :]
    cw = cw_ref[...]
    y = u2 * cw[0:1, :] + u1 * cw[1:2, :] + u * cw[2:3, :]
    a_ref[...] = (gate_b * y * (z * jax.nn.sigmoid(z))).astype(bf16)


def _conv_in(x, g, w_in, conv_w, *, ts):
    s, d = x.shape
    kern = functools.partial(_conv_in_kernel, ts=ts, d=d)
    return pl.pallas_call(
        kern,
        grid=(s // ts,),
        in_specs=[
            pl.BlockSpec((ts, d), lambda i: (i, 0)),
            _const_spec((1, d)),
            _const_spec((d, 4 * d)),
            _const_spec((8, d)),
        ],
        out_specs=pl.BlockSpec((ts, d), lambda i: (i, 0)),
        out_shape=jax.ShapeDtypeStruct((s, d), bf16),
        scratch_shapes=[pltpu.VMEM((8, d), f32)],
        compiler_params=pltpu.CompilerParams(
            dimension_semantics=("arbitrary",), vmem_limit_bytes=VMEM_LIMIT_BYTES),
        name="conv_in",
    )(x, g, w_in, conv_w)


def _mix_out_kernel(*refs, transposed):
    if transposed:
        (x_ref, ot_ref, gate_ref, wo_ref, gpost_ref, gpe_ref, wg_ref, p_ref, wp_ref, o_ref) = refs
        a = (ot_ref[...].astype(f32).T * gate_ref[...].astype(f32)).astype(bf16)
    else:
        (x_ref, a_ref, wo_ref, gpost_ref, gpe_ref, wg_ref, p_ref, wp_ref, o_ref) = refs
        a = a_ref[...]
    y = jnp.dot(a, wo_ref[...], preferred_element_type=f32)
    x1 = x_ref[...] + _rmsnorm(y, gpost_ref[...])
    hg = _rmsnorm(x1, gpe_ref[...]).astype(bf16)
    gate = jax.nn.sigmoid(jnp.dot(hg, wg_ref[...], preferred_element_type=f32))
    pp = jnp.dot(p_ref[...].astype(bf16), wp_ref[...], preferred_element_type=f32)
    o_ref[...] = x1 + gate * pp


def _mix_out(x, acts, w_out, g_post, g_pe, w_gate, p_i, w_proj, *, ts, transposed):
    s, d = x.shape
    ple = p_i.shape[1]
    row_spec = pl.BlockSpec((ts, d), lambda i: (i, 0))
    if transposed:
        act_specs = [pl.BlockSpec((d, ts), lambda i: (0, i)), row_spec]
    else:
        act_specs = [row_spec]
    kern = functools.partial(_mix_out_kernel, transposed=transposed)
    return pl.pallas_call(
        kern,
        grid=(s // ts,),
        in_specs=[row_spec] + act_specs + [
            _const_spec((d, d)),
            _const_spec((1, d)),
            _const_spec((1, d)),
            _const_spec((d, d)),
            pl.BlockSpec((ts, ple), lambda i: (i, 0)),
            _const_spec((ple, d)),
        ],
        out_specs=row_spec,
        out_shape=jax.ShapeDtypeStruct((s, d), f32),
        compiler_params=pltpu.CompilerParams(
            dimension_semantics=("arbitrary",), vmem_limit_bytes=VMEM_LIMIT_BYTES),
        name="mix_out_t" if transposed else "mix_out",
    )(x, *acts, w_out, g_post, g_pe, w_gate, p_i, w_proj)


def _tile(s, want):
    return math.gcd(s, want)


def kernel(x, p, attn_w_in, attn_b_f, attn_w_out, conv_w_in, conv_kernel, conv_w_out,
           norm_pre, norm_post, pe_norm, pe_w_gate, pe_w_proj):
    b, s, d = x.shape
    assert b == 1 and d == N_HEADS * HEAD_DIM
    depth = p.shape[0]
    ts = _tile(s, 512)
    tq = _tile(s, 1024)
    tk = _tile(s, 256)
    xs = x.reshape(s, d)
    for i in range(depth):
        j = i // 2
        g_pre = norm_pre[i].reshape(1, d)
        g_post = norm_post[i].reshape(1, d)
        g_pe = pe_norm[i].reshape(1, d)
        w_gate = pe_w_gate[i].astype(bf16)
        w_proj = pe_w_proj[i].astype(bf16)
        if i % 2 == 0:
            w_in = attn_w_in[j]
            w_main = w_in[:, :4 * d].astype(bf16)
            w_f = jnp.pad(w_in[:, 4 * d:], ((0, 0), (0, LANES - N_HEADS))).astype(bf16)
            b_f = jnp.pad(attn_b_f[j], (0, LANES - N_HEADS)).reshape(1, LANES)
            qta, ka, vta, gate = _attn_in(xs, g_pre, w_main, w_f, b_f, ts=ts)
            ot = _fox_attn(qta, ka, vta, tq=tq, tk=tk)
            xs = _mix_out(xs, (ot, gate), attn_w_out[j].astype(bf16), g_post, g_pe,
                          w_gate, p[i, 0], w_proj, ts=ts, transposed=True)
        else:
            cw = jnp.pad(conv_kernel[j], ((0, 8 - CONV_TAPS), (0, 0)))
            a = _conv_in(xs, g_pre, conv_w_in[j].astype(bf16), cw, ts=ts)
            xs = _mix_out(xs, (a,), conv_w_out[j].astype(bf16), g_post, g_pe,
                          w_gate, p[i, 0], w_proj, ts=ts, transposed=False)
    return xs.reshape(b, s, d)
```

```python
import functools
import math

import jax
import jax.numpy as jnp
from jax import lax
from jax.experimental import pallas as pl
from jax.experimental.pallas import tpu as pltpu

N_HEADS = 16
HEAD_DIM = 64
CONV_TAPS = 3
RMS_EPS = 1e-6
LOG2E = math.log2(math.e)

LANES = 128
SUBLANES = 8
BF16_ROWS = 2 * SUBLANES
MXU_COLS = 256
PITCH_PAD = LANES
AUG_ROWS = 16
VT_ROWS = HEAD_DIM + AUG_ROWS
VMEM_LIMIT_BYTES = 56 * 1024 * 1024
MASK_VALUE = -(2.0 ** 100)

f32 = jnp.float32
bf16 = jnp.bfloat16


def _rmsnorm(x, g):
    ms = jnp.mean(x * x, axis=-1, keepdims=True)
    return x * lax.rsqrt(ms + RMS_EPS) * g


def _split3(c):
    hi = c.astype(bf16).astype(f32)
    r = c - hi
    mid = r.astype(bf16).astype(f32)
    lo = (r - mid).astype(bf16).astype(f32)
    return hi, mid, lo


def _log_sigmoid(x):
    return jnp.minimum(x, 0.0) - jnp.log1p(jnp.exp(-jnp.abs(x)))


def _const_spec(shape):
    return pl.BlockSpec(shape, lambda *_: (0,) * len(shape))


def _attn_in_kernel(x_ref, g_ref, w_ref, wf_ref, bf_ref,
                    qta_ref, ka_ref, vta_ref, gate_ref, carry_ref, *, ts, d):
    @pl.when(pl.program_id(0) == 0)
    def _():
        carry_ref[...] = jnp.zeros_like(carry_ref)

    h = _rmsnorm(x_ref[...], g_ref[...]).astype(bf16)
    proj = jnp.dot(h, w_ref[...], preferred_element_type=f32)
    f_logit = jnp.dot(h, wf_ref[...], preferred_element_type=f32) + bf_ref[...]
    logf2 = _log_sigmoid(f_logit) * LOG2E

    rows = lax.broadcasted_iota(jnp.int32, (ts, LANES), 0)
    c = logf2
    shift = 1
    while shift < ts:
        c = c + jnp.where(rows >= shift, pltpu.roll(c, shift, axis=0), 0.0)
        shift *= 2
    c = c + carry_ref[...]
    carry_ref[...] = c[ts - 1:ts, :]
    c_t = c.T

    q_scale = LOG2E / math.sqrt(HEAD_DIM)
    lane = lax.broadcasted_iota(jnp.int32, (ts, LANES), 1)
    row = lax.broadcasted_iota(jnp.int32, (HEAD_DIM, ts), 0)
    vrow = lax.broadcasted_iota(jnp.int32, (AUG_ROWS, ts), 0)
    ones_blk = jnp.where(vrow == 0, 1.0, 0.0).astype(bf16)
    for pair in range(N_HEADS // 2):
        lo_col = pair * LANES
        q_t = (proj[:, lo_col:lo_col + LANES] * q_scale).T
        k_pair = proj[:, d + lo_col:d + lo_col + LANES]
        k_swap = pltpu.roll(k_pair, HEAD_DIM, axis=1)
        v_t = proj[:, 2 * d + lo_col:2 * d + lo_col + LANES].T
        for e in range(2):
            hd = 2 * pair + e
            hi, mid, lo = _split3(jnp.broadcast_to(c[:, hd:hd + 1], (ts, LANES)))
            k_h = k_pair if e == 0 else k_swap
            ka = jnp.where(lane < HEAD_DIM, k_h,
                 jnp.where(lane < HEAD_DIM + 3, 1.0,
                 jnp.where(lane == HEAD_DIM + 3, -hi,
                 jnp.where(lane == HEAD_DIM + 4, -mid,
                 jnp.where(lane == HEAD_DIM + 5, -lo, 0.0)))))
            ka_ref[:, hd * LANES:(hd + 1) * LANES] = ka.astype(bf16)
            hi, mid, lo = _split3(jnp.broadcast_to(c_t[hd:hd + 1, :], (HEAD_DIM, ts)))
            aug = jnp.where(row == 0, hi,
                  jnp.where(row == 1, mid,
                  jnp.where(row == 2, lo,
                  jnp.where(row < 6, 1.0, 0.0))))
            base = hd * 2 * HEAD_DIM
            qta_ref[base:base + HEAD_DIM, :] = q_t[e * HEAD_DIM:(e + 1) * HEAD_DIM].astype(bf16)
            qta_ref[base + HEAD_DIM:base + 2 * HEAD_DIM, :] = aug.astype(bf16)
            vbase = hd * VT_ROWS
            vta_ref[vbase:vbase + HEAD_DIM, :] = v_t[e * HEAD_DIM:(e + 1) * HEAD_DIM].astype(bf16)
            vta_ref[vbase + HEAD_DIM:vbase + VT_ROWS, :] = ones_blk
    z = proj[:, 3 * d:4 * d]
    gate_ref[...] = (z * jax.nn.sigmoid(z)).astype(bf16)


def _attn_in(x, g, w_main, w_f, b_f, *, ts):
    s, d = x.shape
    kern = functools.partial(_attn_in_kernel, ts=ts, d=d)
    return pl.pallas_call(
        kern,
        grid=(s // ts,),
        in_specs=[
            pl.BlockSpec((ts, d), lambda i: (i, 0)),
            _const_spec((1, d)),
            _const_spec((d, 4 * d)),
            _const_spec((d, LANES)),
            _const_spec((1, LANES)),
        ],
        out_specs=[
            pl.BlockSpec((N_HEADS * 2 * HEAD_DIM, ts), lambda i: (0, i)),
            pl.BlockSpec((ts, N_HEADS * LANES), lambda i: (i, 0)),
            pl.BlockSpec((N_HEADS * VT_ROWS, ts), lambda i: (0, i)),
            pl.BlockSpec((ts, d), lambda i: (i, 0)),
        ],
        out_shape=[
            jax.ShapeDtypeStruct((N_HEADS * 2 * HEAD_DIM, s), bf16),
            jax.ShapeDtypeStruct((s, N_HEADS * LANES), bf16),
            jax.ShapeDtypeStruct((N_HEADS * VT_ROWS, s), bf16),
            jax.ShapeDtypeStruct((s, d), bf16),
        ],
        scratch_shapes=[pltpu.VMEM((1, LANES), f32)],
        compiler_params=pltpu.CompilerParams(
            dimension_semantics=("arbitrary",), vmem_limit_bytes=VMEM_LIMIT_BYTES),
        name="attn_in",
    )(x, g, w_main, w_f, b_f)


def _fox_attn_kernel(qta_ref, ka_ref, vta_ref, ot_ref, *scratch, tq, tk):
    i = pl.program_id(1)
    nsub = tq // tk
    s_buf, p_buf, alpha_buf, cmax_buf = (scratch[k * nsub:(k + 1) * nsub] for k in range(4))
    m_ref, acc_ref = scratch[4 * nsub:]

    def column_max(s_g):
        m8 = jnp.max(s_g.reshape(s_g.shape[0] // SUBLANES, SUBLANES, LANES), axis=0)
        for sh in (4, 2, 1):
            m8 = jnp.maximum(m8, pltpu.roll(m8, sh, axis=0))
        return m8

    def scores(c, slot, col0=0):
        start = pl.multiple_of(c * tk, tk)
        s_t = jnp.dot(ka_ref[pl.ds(start, tk), :], qta_ref[:, col0:],
                      preferred_element_type=f32)
        for n0 in range(col0, tq, LANES):
            s_g = s_t[:, n0 - col0:n0 - col0 + LANES]
            cmax_buf[slot][:, n0:n0 + LANES] = column_max(s_g)
            s_buf[slot][:, n0:n0 + LANES] = s_g

    def softmax(slot, col0=0, diagonal=False):
        for g in range(col0 // LANES, tq // LANES):
            cols = slice(g * LANES, (g + 1) * LANES)
            masked = diagonal and g * LANES < col0 + tk

            def visible(row0, rows):
                r = lax.broadcasted_iota(jnp.int32, (rows, LANES), 0) + row0
                col = lax.broadcasted_iota(jnp.int32, (rows, LANES), 1) + (g * LANES - col0)
                return r <= col

            if masked:
                s_g = s_buf[slot][:, cols]
                c_max = column_max(jnp.where(visible(0, tk), s_g, MASK_VALUE))
            else:
                c_max = cmax_buf[slot][:, cols]
            m_prev = m_ref[:, cols]
            m_new = jnp.maximum(m_prev, c_max)
            alpha_buf[slot][:, cols] = jnp.exp2(m_prev - m_new)
            m_ref[:, cols] = m_new
            for r0 in range(0, tk, BF16_ROWS):
                s_r = s_buf[slot][r0:r0 + BF16_ROWS, cols]
                if masked:
                    s_r = jnp.where(visible(r0, BF16_ROWS), s_r, MASK_VALUE)
                x = s_r.reshape(2, SUBLANES, LANES) - m_new[None]
                p_buf[slot][r0:r0 + BF16_ROWS, cols] = jnp.exp2(
                    x.reshape(BF16_ROWS, LANES).astype(bf16))

    def accumulate(c, slot, col0=0):
        start = pl.multiple_of(c * tk, tk)
        pv = jnp.dot(vta_ref[:, pl.ds(start, tk)], p_buf[slot][:, col0:tq],
                     preferred_element_type=f32)
        for n0 in range(col0, tq, MXU_COLS):
            cols = slice(n0, n0 + MXU_COLS)
            acc3 = acc_ref[:, cols].reshape(VT_ROWS // SUBLANES, SUBLANES, MXU_COLS)
            pv3 = pv[:, n0 - col0:n0 - col0 + MXU_COLS].reshape(acc3.shape)
            acc3 = alpha_buf[slot][:, cols][None] * acc3 + pv3
            acc_ref[:, cols] = acc3.reshape(VT_ROWS, MXU_COLS)

    m_ref[...] = jnp.full_like(m_ref, MASK_VALUE)
    acc_ref[:, :tq] = jnp.zeros((VT_ROWS, tq), f32)
    p_buf[nsub - 1][:, :tq] = jnp.zeros((tk, tq), bf16)
    alpha_buf[nsub - 1][...] = jnp.ones_like(alpha_buf[nsub - 1])
    scores(0, 0)

    def body(ii, carry):
        for u in range(nsub):
            c = ii * nsub + u
            softmax(u)
            accumulate(jnp.maximum(c - 1, 0), (u - 1) % nsub)
            scores(c + 1, (u + 1) % nsub)
        return carry
    lax.fori_loop(0, i, body, 0)

    first = i * nsub
    for d in range(nsub):
        if d + 1 < nsub:
            scores(first + d + 1, d + 1, (d + 1) * tk)
        softmax(d, d * tk, diagonal=True)
        accumulate(jnp.maximum(first + d - 1, 0), (d - 1) % nsub, max(d - 1, 0) * tk)
    accumulate(first + nsub - 1, nsub - 1, (nsub - 1) * tk)

    acc = acc_ref[:, :tq]
    inv_l = 1.0 / acc[HEAD_DIM:HEAD_DIM + 1, :]
    ot_ref[...] = (acc[:HEAD_DIM, :] * inv_l).astype(bf16)


def _fox_attn(qta, ka, vta, *, tq, tk):
    s = ka.shape[0]
    nsub = tq // tk
    assert nsub >= 2
    kern = functools.partial(_fox_attn_kernel, tq=tq, tk=tk)
    return pl.pallas_call(
        kern,
        grid=(N_HEADS, s // tq),
        in_specs=[
            pl.BlockSpec((2 * HEAD_DIM, tq), lambda h, i: (h, i)),
            pl.BlockSpec((s, LANES), lambda h, i: (0, h)),
            pl.BlockSpec((VT_ROWS, s), lambda h, i: (h, 0)),
        ],
        out_specs=pl.BlockSpec((HEAD_DIM, tq), lambda h, i: (h, i)),
        out_shape=jax.ShapeDtypeStruct((N_HEADS * HEAD_DIM, s), bf16),
        scratch_shapes=(
            [pltpu.VMEM((tk, tq + PITCH_PAD), f32)] * nsub
            + [pltpu.VMEM((tk, tq + PITCH_PAD), bf16)] * nsub
            + [pltpu.VMEM((SUBLANES, tq), f32)] * nsub
            + [pltpu.VMEM((SUBLANES, tq), f32)] * nsub
            + [pltpu.VMEM((SUBLANES, tq), f32),
               pltpu.VMEM((VT_ROWS, tq + PITCH_PAD), f32)]
        ),
        compiler_params=pltpu.CompilerParams(
            dimension_semantics=("arbitrary", "arbitrary"),
            vmem_limit_bytes=VMEM_LIMIT_BYTES),
        name="fox_attn",
    )(qta, ka, vta)


def _conv_in_kernel(x_ref, g_ref, w_ref, cw_ref, a_ref, tail_ref, *, ts, d):
    @pl.when(pl.program_id(0) == 0)
    def _():
        tail_ref[...] = jnp.zeros_like(tail_ref)

    h = _rmsnorm(x_ref[...], g_ref[...]).astype(bf16)
    proj = jnp.dot(h, w_ref[...], preferred_element_type=f32)
    gate_b = proj[:, 0:d]
    u = proj[:, d:2 * d] * proj[:, 2 * d:3 * d]
    z = proj[:, 3 * d:4 * d]

    rows = lax.broadcasted_iota(jnp.int32, (ts, d), 0)
    tail = tail_ref[...]
    prev2 = jnp.broadcast_to(tail[0:1, :], (ts, d))
    prev1 = jnp.broadcast_to(tail[1:2, :], (ts, d))
    u1 = jnp.where(rows == 0, prev1, pltpu.roll(u, 1, axis=0))
    u2 = jnp.where(rows == 0, prev2, jnp.where(rows == 1, prev1, pltpu.roll(u, 2, axis=0)))
    tail_ref[0:2, :] = u[ts - 2:ts, :]
    cw = cw_ref[...]
    y = u2 * cw[0:1, :] + u1 * cw[1:2, :] + u * cw[2:3, :]
    a_ref[...] = (gate_b * y * (z * jax.nn.sigmoid(z))).astype(bf16)


def _conv_in(x, g, w_in, conv_w, *, ts):
    s, d = x.shape
    kern = functools.partial(_conv_in_kernel, ts=ts, d=d)
    return pl.pallas_call(
        kern,
        grid=(s // ts,),
        in_specs=[
            pl.BlockSpec((ts, d), lambda i: (i, 0)),
            _const_spec((1, d)),
            _const_spec((d, 4 * d)),
            _const_spec((8, d)),
        ],
        out_specs=pl.BlockSpec((ts, d), lambda i: (i, 0)),
        out_shape=jax.ShapeDtypeStruct((s, d), bf16),
        scratch_shapes=[pltpu.VMEM((8, d), f32)],
        compiler_params=pltpu.CompilerParams(
            dimension_semantics=("arbitrary",), vmem_limit_bytes=VMEM_LIMIT_BYTES),
        name="conv_in",
    )(x, g, w_in, conv_w)


def _mix_out_kernel(*refs, transposed):
    if transposed:
        (x_ref, ot_ref, gate_ref, wo_ref, gpost_ref, gpe_ref, wg_ref, p_ref, wp_ref, o_ref) = refs
        a = (ot_ref[...].astype(f32).T * gate_ref[...].astype(f32)).astype(bf16)
    else:
        (x_ref, a_ref, wo_ref, gpost_ref, gpe_ref, wg_ref, p_ref, wp_ref, o_ref) = refs
        a = a_ref[...]
    y = jnp.dot(a, wo_ref[...], preferred_element_type=f32)
    x1 = x_ref[...] + _rmsnorm(y, gpost_ref[...])
    hg = _rmsnorm(x1, gpe_ref[...]).astype(bf16)
    gate = jax.nn.sigmoid(jnp.dot(hg, wg_ref[...], preferred_element_type=f32))
    pp = jnp.dot(p_ref[...].astype(bf16), wp_ref[...], preferred_element_type=f32)
    o_ref[...] = x1 + gate * pp


def _mix_out(x, acts, w_out, g_post, g_pe, w_gate, p_i, w_proj, *, ts, transposed):
    s, d = x.shape
    ple = p_i.shape[1]
    row_spec = pl.BlockSpec((ts, d), lambda i: (i, 0))
    if transposed:
        act_specs = [pl.BlockSpec((d, ts), lambda i: (0, i)), row_spec]
    else:
        act_specs = [row_spec]
    kern = functools.partial(_mix_out_kernel, transposed=transposed)
    return pl.pallas_call(
        kern,
        grid=(s // ts,),
        in_specs=[row_spec] + act_specs + [
            _const_spec((d, d)),
            _const_spec((1, d)),
            _const_spec((1, d)),
            _const_spec((d, d)),
            pl.BlockSpec((ts, ple), lambda i: (i, 0)),
            _const_spec((ple, d)),
        ],
        out_specs=row_spec,
        out_shape=jax.ShapeDtypeStruct((s, d), f32),
        compiler_params=pltpu.CompilerParams(
            dimension_semantics=("arbitrary",), vmem_limit_bytes=VMEM_LIMIT_BYTES),
        name="mix_out_t" if transposed else "mix_out",
    )(x, *acts, w_out, g_post, g_pe, w_gate, p_i, w_proj)


def _tile(s, want):
    return math.gcd(s, want)


def kernel(x, p, attn_w_in, attn_b_f, attn_w_out, conv_w_in, conv_kernel, conv_w_out,
           norm_pre, norm_post, pe_norm, pe_w_gate, pe_w_proj):
    b, s, d = x.shape
    assert b == 1 and d == N_HEADS * HEAD_DIM
    depth = p.shape[0]
    ts = _tile(s, 512)
    tq = _tile(s, 1024)
    tk = _tile(s, 256)
    xs = x.reshape(s, d)
    for i in range(depth):
        j = i // 2
        g_pre = norm_pre[i].reshape(1, d)
        g_post = norm_post[i].reshape(1, d)
        g_pe = pe_norm[i].reshape(1, d)
        w_gate = pe_w_gate[i].astype(bf16)
        w_proj = pe_w_proj[i].astype(bf16)
        if i % 2 == 0:
            w_in = attn_w_in[j]
            w_main = w_in[:, :4 * d].astype(bf16)
            w_f = jnp.pad(w_in[:, 4 * d:], ((0, 0), (0, LANES - N_HEADS))).astype(bf16)
            b_f = jnp.pad(attn_b_f[j], (0, LANES - N_HEADS)).reshape(1, LANES)
            qta, ka, vta, gate = _attn_in(xs, g_pre, w_main, w_f, b_f, ts=ts)
            ot = _fox_attn(qta, ka, vta, tq=tq, tk=tk)
            xs = _mix_out(xs, (ot, gate), attn_w_out[j].astype(bf16), g_post, g_pe,
                          w_gate, p[i, 0], w_proj, ts=ts, transposed=True)
        else:
            cw = jnp.pad(conv_kernel[j], ((0, 8 - CONV_TAPS), (0, 0)))
            a = _conv_in(xs, g_pre, conv_w_in[j].astype(bf16), cw, ts=ts)
            xs = _mix_out(xs, (a,), conv_w_out[j].astype(bf16), g_post, g_pe,
                          w_gate, p[i, 0], w_proj, ts=ts, transposed=False)
    return xs.reshape(b, s, d)
```

```python
import functools
import math

import jax
import jax.numpy as jnp
from jax import lax
from jax.experimental import pallas as pl
from jax.experimental.pallas import tpu as pltpu

N_HEADS = 16
HEAD_DIM = 64
CONV_TAPS = 3
RMS_EPS = 1e-6
LOG2E = math.log2(math.e)

LANES = 128
SUBLANES = 8
BF16_ROWS = 2 * SUBLANES
MXU_COLS = 256
PITCH_PAD = LANES
AUG_ROWS = 16
VT_ROWS = HEAD_DIM + AUG_ROWS
C_LANES = (80, 96, 112)
VMEM_LIMIT_BYTES = 56 * 1024 * 1024
MASK_VALUE = -(2.0 ** 100)

f32 = jnp.float32
bf16 = jnp.bfloat16


def _rmsnorm(x, g):
    ms = jnp.mean(x * x, axis=-1, keepdims=True)
    return x * lax.rsqrt(ms + RMS_EPS) * g


def _split3(c):
    hi = c.astype(bf16).astype(f32)
    r = c - hi
    mid = r.astype(bf16).astype(f32)
    lo = (r - mid).astype(bf16).astype(f32)
    return hi, mid, lo


def _log_sigmoid(x):
    return jnp.minimum(x, 0.0) - jnp.log1p(jnp.exp(-jnp.abs(x)))


def _const_spec(shape):
    return pl.BlockSpec(shape, lambda *_: (0,) * len(shape))


def _layer_spec(shape, layer):
    return pl.BlockSpec((None,) + shape, lambda *_: (layer,) + (0,) * len(shape))


def _attn_in_kernel(x_ref, g_ref, w_ref, wf_ref, bf_ref,
                    qta_ref, ka_ref, vta_ref, gate_ref, carry_ref, *, ts, d):
    @pl.when(pl.program_id(0) == 0)
    def _():
        carry_ref[...] = jnp.zeros_like(carry_ref)

    h = _rmsnorm(x_ref[...], g_ref[...]).astype(bf16)
    proj = jnp.dot(h, w_ref[...], preferred_element_type=f32)
    f_logit = jnp.dot(h, wf_ref[...], preferred_element_type=f32) + bf_ref[...]
    logf2 = _log_sigmoid(f_logit) * LOG2E

    rows = lax.broadcasted_iota(jnp.int32, (ts, LANES), 0)
    c = logf2
    shift = 1
    while shift < ts:
        c = c + jnp.where(rows >= shift, pltpu.roll(c, shift, axis=0), 0.0)
        shift *= 2
    c = c + carry_ref[...]
    carry_ref[...] = c[ts - 1:ts, :]
    c_t = c.T

    q_scale = LOG2E / math.sqrt(HEAD_DIM)
    lane = lax.broadcasted_iota(jnp.int32, (ts, LANES), 1)
    vrow = lax.broadcasted_iota(jnp.int32, (AUG_ROWS, ts), 0)
    ones_blk = jnp.where(vrow == 0, 1.0, 0.0).astype(bf16)

    n_hi, n_mid, n_lo = (-piece for piece in _split3(c))
    c_pieces = jnp.where(lane < N_HEADS, n_hi,
               jnp.where(lane < 2 * N_HEADS, pltpu.roll(n_mid, N_HEADS, axis=1),
                         pltpu.roll(n_lo, 2 * N_HEADS, axis=1)))
    is_c_lane = (lane == C_LANES[0]) | (lane == C_LANES[1]) | (lane == C_LANES[2])
    ones_lanes = jnp.where((lane >= HEAD_DIM) & (lane < HEAD_DIM + 3), 1.0, 0.0)
    q_hi, q_mid, q_lo = _split3(c_t[:N_HEADS])
    arow = lax.broadcasted_iota(jnp.int32, (AUG_ROWS, ts), 0)
    crow = lax.broadcasted_iota(jnp.int32, (HEAD_DIM - AUG_ROWS, ts), 0) + (HEAD_DIM + AUG_ROWS)
    q_ones = jnp.where((crow == C_LANES[0]) | (crow == C_LANES[1]) | (crow == C_LANES[2]),
                       1.0, 0.0).astype(bf16)
    for pair in range(N_HEADS // 2):
        lo_col = pair * LANES
        q_t = (proj[:, lo_col:lo_col + LANES] * q_scale).T
        k_pair = proj[:, d + lo_col:d + lo_col + LANES]
        k_swap = pltpu.roll(k_pair, HEAD_DIM, axis=1)
        v_t = proj[:, 2 * d + lo_col:2 * d + lo_col + LANES].T
        for e in range(2):
            hd = 2 * pair + e
            k_h = k_pair if e == 0 else k_swap
            c_h = pltpu.roll(c_pieces, C_LANES[0] - hd, axis=1)
            ka = jnp.where(lane < HEAD_DIM, k_h, jnp.where(is_c_lane, c_h, ones_lanes))
            ka_ref[:, hd * LANES:(hd + 1) * LANES] = ka.astype(bf16)
            aug = jnp.where(arow == 0, jnp.broadcast_to(q_hi[hd:hd + 1], (AUG_ROWS, ts)),
                  jnp.where(arow == 1, jnp.broadcast_to(q_mid[hd:hd + 1], (AUG_ROWS, ts)),
                  jnp.where(arow == 2, jnp.broadcast_to(q_lo[hd:hd + 1], (AUG_ROWS, ts)), 0.0)))
            base = hd * 2 * HEAD_DIM
            qta_ref[base:base + HEAD_DIM, :] = q_t[e * HEAD_DIM:(e + 1) * HEAD_DIM].astype(bf16)
            qta_ref[base + HEAD_DIM:base + HEAD_DIM + AUG_ROWS, :] = aug.astype(bf16)
            qta_ref[base + HEAD_DIM + AUG_ROWS:base + 2 * HEAD_DIM, :] = q_ones
            vbase = hd * VT_ROWS
            vta_ref[vbase:vbase + HEAD_DIM, :] = v_t[e * HEAD_DIM:(e + 1) * HEAD_DIM].astype(bf16)
            vta_ref[vbase + HEAD_DIM:vbase + VT_ROWS, :] = ones_blk
    z = proj[:, 3 * d:4 * d]
    gate_ref[...] = (z * jax.nn.sigmoid(z)).astype(bf16)


def _attn_in(x, g, w_in, layer, w_f, b_f, *, ts):
    s, d = x.shape
    kern = functools.partial(_attn_in_kernel, ts=ts, d=d)
    return pl.pallas_call(
        kern,
        grid=(s // ts,),
        in_specs=[
            pl.BlockSpec((ts, d), lambda i: (i, 0)),
            _const_spec((1, d)),
            _layer_spec((d, 4 * d), layer),
            _const_spec((d, LANES)),
            _const_spec((1, LANES)),
        ],
        out_specs=[
            pl.BlockSpec((N_HEADS * 2 * HEAD_DIM, ts), lambda i: (0, i)),
            pl.BlockSpec((ts, N_HEADS * LANES), lambda i: (i, 0)),
            pl.BlockSpec((N_HEADS * VT_ROWS, ts), lambda i: (0, i)),
            pl.BlockSpec((ts, d), lambda i: (i, 0)),
        ],
        out_shape=[
            jax.ShapeDtypeStruct((N_HEADS * 2 * HEAD_DIM, s), bf16),
            jax.ShapeDtypeStruct((s, N_HEADS * LANES), bf16),
            jax.ShapeDtypeStruct((N_HEADS * VT_ROWS, s), bf16),
            jax.ShapeDtypeStruct((s, d), bf16),
        ],
        scratch_shapes=[pltpu.VMEM((1, LANES), f32)],
        compiler_params=pltpu.CompilerParams(
            dimension_semantics=("arbitrary",), vmem_limit_bytes=VMEM_LIMIT_BYTES),
        name="attn_in",
    )(x, g, w_in, w_f, b_f)


def _fox_attn_kernel(qta_ref, ka_ref, vta_ref, ot_ref, *scratch, tq, tk):
    i = pl.program_id(1)
    nsub = tq // tk
    s_buf, p_buf, alpha_buf, cmax_buf = (scratch[k * nsub:(k + 1) * nsub] for k in range(4))
    m_ref, acc_ref, qa_ref = scratch[4 * nsub:]

    def column_max(s_g):
        m8 = jnp.max(s_g.reshape(s_g.shape[0] // SUBLANES, SUBLANES, LANES), axis=0)
        for sh in (4, 2, 1):
            m8 = jnp.maximum(m8, pltpu.roll(m8, sh, axis=0))
        return m8

    def scores(c, slot, col0=0, col1=tq):
        start = pl.multiple_of(c * tk, tk)
        s_t = jnp.dot(ka_ref[pl.ds(start, tk), :], qa_ref[:, col0:col1],
                      preferred_element_type=f32)
        for n0 in range(col0, col1, LANES):
            s_g = s_t[:, n0 - col0:n0 - col0 + LANES]
            cmax_buf[slot][:, n0:n0 + LANES] = column_max(s_g)
            s_buf[slot][:, n0:n0 + LANES] = s_g

    def softmax(slot, col0=0, col1=tq, diagonal=False):
        for g in range(col0 // LANES, col1 // LANES):
            cols = slice(g * LANES, (g + 1) * LANES)
            masked = diagonal and g * LANES < col0 + tk

            def visible(row0, rows):
                r = lax.broadcasted_iota(jnp.int32, (rows, LANES), 0) + row0
                col = lax.broadcasted_iota(jnp.int32, (rows, LANES), 1) + (g * LANES - col0)
                return r <= col

            if masked:
                s_g = s_buf[slot][:, cols]
                c_max = column_max(jnp.where(visible(0, tk), s_g, MASK_VALUE))
            else:
                c_max = cmax_buf[slot][:, cols]
            m_prev = m_ref[:, cols]
            m_new = jnp.maximum(m_prev, c_max)
            alpha_buf[slot][:, cols] = jnp.exp2(m_prev - m_new)
            m_ref[:, cols] = m_new
            for r0 in range(0, tk, BF16_ROWS):
                s_r = s_buf[slot][r0:r0 + BF16_ROWS, cols]
                if masked:
                    s_r = jnp.where(visible(r0, BF16_ROWS), s_r, MASK_VALUE)
                x = s_r.reshape(2, SUBLANES, LANES) - m_new[None]
                p_buf[slot][r0:r0 + BF16_ROWS, cols] = jnp.exp2(
                    x.reshape(BF16_ROWS, LANES).astype(bf16))

    def accumulate(c, slot, col0=0, col1=tq):
        start = pl.multiple_of(c * tk, tk)
        pv = jnp.dot(vta_ref[:, pl.ds(start, tk)], p_buf[slot][:, col0:col1],
                     preferred_element_type=f32)
        for n0 in range(col0, col1, MXU_COLS):
            cols = slice(n0, n0 + MXU_COLS)
            acc3 = acc_ref[:, cols].reshape(VT_ROWS // SUBLANES, SUBLANES, MXU_COLS)
            pv3 = pv[:, n0 - col0:n0 - col0 + MXU_COLS].reshape(acc3.shape)
            acc3 = alpha_buf[slot][:, cols][None] * acc3 + pv3
            acc_ref[:, cols] = acc3.reshape(VT_ROWS, MXU_COLS)

    m_ref[...] = jnp.full_like(m_ref, MASK_VALUE)
    acc_ref[:, :tq] = jnp.zeros((VT_ROWS, tq), f32)
    p_buf[nsub - 1][:, :tq] = jnp.zeros((tk, tq), bf16)
    alpha_buf[nsub - 1][...] = jnp.ones_like(alpha_buf[nsub - 1])
    qa_ref[:, :tq] = qta_ref[...]
    scores(0, 0)

    def group(base):
        for u in range(nsub):
            c = base + u
            softmax(u)
            accumulate(jnp.maximum(c - 1, 0), (u - 1) % nsub)
            scores(c + 1, (u + 1) % nsub)

    odd = i & 1

    @pl.when(odd == 1)
    def _():
        group(0)

    def body(jj, carry):
        base = (odd + 2 * jj) * nsub
        group(base)
        group(base + nsub)
        return carry
    lax.fori_loop(0, i // 2, body, 0)

    first = i * nsub
    for d in range(nsub):
        if d + 1 < nsub:
            scores(first + d + 1, d + 1, (d + 1) * tk)
        softmax(d, d * tk, tq, diagonal=True)
        accumulate(jnp.maximum(first + d - 1, 0), (d - 1) % nsub, max(d - 1, 0) * tk)
    accumulate(first + nsub - 1, nsub - 1, (nsub - 1) * tk)

    acc = acc_ref[:, :tq]
    inv_l = 1.0 / acc[HEAD_DIM:HEAD_DIM + 1, :]
    ot_ref[...] = (acc[:HEAD_DIM, :] * inv_l).astype(bf16)


def _fox_attn(qta, ka, vta, *, tq, tk):
    s = ka.shape[0]
    nsub = tq // tk
    assert nsub >= 2
    kern = functools.partial(_fox_attn_kernel, tq=tq, tk=tk)
    return pl.pallas_call(
        kern,
        grid=(N_HEADS, s // tq),
        in_specs=[
            pl.BlockSpec((2 * HEAD_DIM, tq), lambda h, i: (h, i)),
            pl.BlockSpec((s, LANES), lambda h, i: (0, h)),
            pl.BlockSpec((VT_ROWS, s), lambda h, i: (h, 0)),
        ],
        out_specs=pl.BlockSpec((HEAD_DIM, tq), lambda h, i: (h, i)),
        out_shape=jax.ShapeDtypeStruct((N_HEADS * HEAD_DIM, s), bf16),
        scratch_shapes=(
            [pltpu.VMEM((tk, tq + PITCH_PAD), f32)] * nsub
            + [pltpu.VMEM((tk, tq + PITCH_PAD), bf16)] * nsub
            + [pltpu.VMEM((SUBLANES, tq), f32)] * nsub
            + [pltpu.VMEM((SUBLANES, tq), f32)] * nsub
            + [pltpu.VMEM((SUBLANES, tq), f32),
               pltpu.VMEM((VT_ROWS, tq + PITCH_PAD), f32),
               pltpu.VMEM((2 * HEAD_DIM, tq + PITCH_PAD), bf16)]
        ),
        compiler_params=pltpu.CompilerParams(
            dimension_semantics=("arbitrary", "arbitrary"),
            vmem_limit_bytes=VMEM_LIMIT_BYTES),
        name="fox_attn",
    )(qta, ka, vta)


def _conv_in_kernel(x_ref, g_ref, w_ref, cw_ref, a_ref, tail_ref, *, ts, d):
    @pl.when(pl.program_id(0) == 0)
    def _():
        tail_ref[...] = jnp.zeros_like(tail_ref)

    h = _rmsnorm(x_ref[...], g_ref[...]).astype(bf16)
    proj = jnp.dot(h, w_ref[...], preferred_element_type=f32)
    gate_b = proj[:, 0:d]
    u = proj[:, d:2 * d] * proj[:, 2 * d:3 * d]
    z = proj[:, 3 * d:4 * d]

    rows = lax.broadcasted_iota(jnp.int32, (ts, d), 0)
    tail = tail_ref[...]
    prev2 = jnp.broadcast_to(tail[0:1, :], (ts, d))
    prev1 = jnp.broadcast_to(tail[1:2, :], (ts, d))
    u1 = jnp.where(rows == 0, prev1, pltpu.roll(u, 1, axis=0))
    u2 = jnp.where(rows == 0, prev2, jnp.where(rows == 1, prev1, pltpu.roll(u, 2, axis=0)))
    tail_ref[0:2, :] = u[ts - 2:ts, :]
    cw = cw_ref[...]
    y = u2 * cw[0:1, :] + u1 * cw[1:2, :] + u * cw[2:3, :]
    a_ref[...] = (gate_b * y * (z * jax.nn.sigmoid(z))).astype(bf16)


def _conv_in(x, g, w_in, layer, conv_w, *, ts):
    s, d = x.shape
    kern = functools.partial(_conv_in_kernel, ts=ts, d=d)
    return pl.pallas_call(
        kern,
        grid=(s // ts,),
        in_specs=[
            pl.BlockSpec((ts, d), lambda i: (i, 0)),
            _const_spec((1, d)),
            _layer_spec((d, 4 * d), layer),
            _const_spec((8, d)),
        ],
        out_specs=pl.BlockSpec((ts, d), lambda i: (i, 0)),
        out_shape=jax.ShapeDtypeStruct((s, d), bf16),
        scratch_shapes=[pltpu.VMEM((8, d), f32)],
        compiler_params=pltpu.CompilerParams(
            dimension_semantics=("arbitrary",), vmem_limit_bytes=VMEM_LIMIT_BYTES),
        name="conv_in",
    )(x, g, w_in, conv_w)


def _mix_out_kernel(*refs, transposed):
    if transposed:
        (x_ref, ot_ref, gate_ref, wo_ref, gpost_ref, gpe_ref, wg_ref, p_ref, wp_ref, o_ref) = refs
        a = (ot_ref[...].astype(f32).T * gate_ref[...].astype(f32)).astype(bf16)
    else:
        (x_ref, a_ref, wo_ref, gpost_ref, gpe_ref, wg_ref, p_ref, wp_ref, o_ref) = refs
        a = a_ref[...]
    y = jnp.dot(a, wo_ref[...], preferred_element_type=f32)
    x1 = x_ref[...] + _rmsnorm(y, gpost_ref[...])
    hg = _rmsnorm(x1, gpe_ref[...]).astype(bf16)
    gate = jax.nn.sigmoid(jnp.dot(hg, wg_ref[...], preferred_element_type=f32))
    pp = jnp.dot(p_ref[...].astype(bf16), wp_ref[...], preferred_element_type=f32)
    o_ref[...] = x1 + gate * pp


def _mix_out(x, acts, w_out, mixer, g_post, g_pe, w_gate, p, w_proj, layer, *, ts, transposed):
    s, d = x.shape
    ple = p.shape[-1]
    row_spec = pl.BlockSpec((ts, d), lambda i: (i, 0))
    if transposed:
        act_specs = [pl.BlockSpec((d, ts), lambda i: (0, i)), row_spec]
    else:
        act_specs = [row_spec]
    kern = functools.partial(_mix_out_kernel, transposed=transposed)
    return pl.pallas_call(
        kern,
        grid=(s // ts,),
        in_specs=[row_spec] + act_specs + [
            _layer_spec((d, d), mixer),
            _const_spec((1, d)),
            _const_spec((1, d)),
            _layer_spec((d, d), layer),
            pl.BlockSpec((None, None, ts, ple), lambda i: (layer, 0, i, 0)),
            _layer_spec((ple, d), layer),
        ],
        out_specs=row_spec,
        out_shape=jax.ShapeDtypeStruct((s, d), f32),
        compiler_params=pltpu.CompilerParams(
            dimension_semantics=("arbitrary",), vmem_limit_bytes=VMEM_LIMIT_BYTES),
        name="mix_out_t" if transposed else "mix_out",
    )(x, *acts, w_out, g_post, g_pe, w_gate, p, w_proj)


def _tile(s, want):
    return math.gcd(s, want)


def kernel(x, p, attn_w_in, attn_b_f, attn_w_out, conv_w_in, conv_kernel, conv_w_out,
           norm_pre, norm_post, pe_norm, pe_w_gate, pe_w_proj):
    b, s, d = x.shape
    assert b == 1 and d == N_HEADS * HEAD_DIM
    depth = p.shape[0]
    ts = _tile(s, 512)
    ts_wide = _tile(s, 1024)
    tq = _tile(s, 1024)
    tk = _tile(s, 256)
    xs = x.reshape(s, d)
    attn_w_in_b = attn_w_in.astype(bf16)
    attn_w_out_b = attn_w_out.astype(bf16)
    conv_w_in_b = conv_w_in.astype(bf16)
    conv_w_out_b = conv_w_out.astype(bf16)
    w_gate_b = pe_w_gate.astype(bf16)
    w_proj_b = pe_w_proj.astype(bf16)
    for i in range(depth):
        j = i // 2
        g_pre = norm_pre[i].reshape(1, d)
        g_post = norm_post[i].reshape(1, d)
        g_pe = pe_norm[i].reshape(1, d)
        if i % 2 == 0:
            w_f = jnp.pad(attn_w_in_b[j, :, 4 * d:], ((0, 0), (0, LANES - N_HEADS)))
            b_f = jnp.pad(attn_b_f[j], (0, LANES - N_HEADS)).reshape(1, LANES)
            qta, ka, vta, gate = _attn_in(xs, g_pre, attn_w_in_b, j, w_f, b_f, ts=ts)
            ot = _fox_attn(qta, ka, vta, tq=tq, tk=tk)
            xs = _mix_out(xs, (ot, gate), attn_w_out_b, j, g_post, g_pe,
                          w_gate_b, p, w_proj_b, i, ts=ts_wide, transposed=True)
        else:
            cw = jnp.pad(conv_kernel[j], ((0, 8 - CONV_TAPS), (0, 0)))
            a = _conv_in(xs, g_pre, conv_w_in_b, j, cw, ts=ts_wide)
            xs = _mix_out(xs, (a,), conv_w_out_b, j, g_post, g_pe,
                          w_gate_b, p, w_proj_b, i, ts=ts_wide, transposed=False)
    return xs.reshape(b, s, d)
```

```python
import functools
import math

import jax
import jax.numpy as jnp
from jax import lax
from jax.experimental import pallas as pl
from jax.experimental.pallas import tpu as pltpu

N_HEADS = 16
HEAD_DIM = 64
CONV_TAPS = 3
RMS_EPS = 1e-6
LOG2E = math.log2(math.e)

LANES = 128
SUBLANES = 8
BF16_ROWS = 2 * SUBLANES
MXU_COLS = 256
PITCH_PAD = LANES
AUG_ROWS = 16
VT_ROWS = HEAD_DIM + AUG_ROWS
ACC_ROWS = HEAD_DIM + 8
C_LANES = (80, 96, 112)
VMEM_LIMIT_BYTES = 56 * 1024 * 1024
MASK_VALUE = -(2.0 ** 100)

f32 = jnp.float32
bf16 = jnp.bfloat16


def _rmsnorm(x, g):
    ms = jnp.mean(x * x, axis=-1, keepdims=True)
    return x * lax.rsqrt(ms + RMS_EPS) * g


def _split3(c):
    hi = c.astype(bf16).astype(f32)
    r = c - hi
    mid = r.astype(bf16).astype(f32)
    lo = (r - mid).astype(bf16).astype(f32)
    return hi, mid, lo


def _log_sigmoid(x):
    return jnp.minimum(x, 0.0) - jnp.log1p(jnp.exp(-jnp.abs(x)))


def _const_spec(shape):
    return pl.BlockSpec(shape, lambda *_: (0,) * len(shape))


def _layer_spec(shape, layer):
    return pl.BlockSpec((None,) + shape, lambda *_: (layer,) + (0,) * len(shape))


def _attn_in_kernel(x_ref, g_ref, w_ref, wf_ref, bf_ref,
                    qta_ref, ka_ref, vta_ref, gate_ref, carry_ref, *, ts, d):
    @pl.when(pl.program_id(0) == 0)
    def _():
        carry_ref[...] = jnp.zeros_like(carry_ref)

    h = _rmsnorm(x_ref[...], g_ref[...]).astype(bf16)
    proj = jnp.dot(h, w_ref[...], preferred_element_type=f32)
    f_logit = jnp.dot(h, wf_ref[...], preferred_element_type=f32) + bf_ref[...]
    logf2 = _log_sigmoid(f_logit) * LOG2E

    rows = lax.broadcasted_iota(jnp.int32, (ts, LANES), 0)
    c = logf2
    shift = 1
    while shift < ts:
        c = c + jnp.where(rows >= shift, pltpu.roll(c, shift, axis=0), 0.0)
        shift *= 2
    c = c + carry_ref[...]
    carry_ref[...] = c[ts - 1:ts, :]
    c_t = c.T

    q_scale = LOG2E / math.sqrt(HEAD_DIM)
    lane = lax.broadcasted_iota(jnp.int32, (ts, LANES), 1)
    vrow = lax.broadcasted_iota(jnp.int32, (AUG_ROWS, ts), 0)
    ones_blk = jnp.where(vrow == 0, 1.0, 0.0).astype(bf16)

    n_hi, n_mid, n_lo = (-piece for piece in _split3(c))
    c_pieces = jnp.where(lane < N_HEADS, n_hi,
               jnp.where(lane < 2 * N_HEADS, pltpu.roll(n_mid, N_HEADS, axis=1),
                         pltpu.roll(n_lo, 2 * N_HEADS, axis=1)))
    is_c_lane = (lane == C_LANES[0]) | (lane == C_LANES[1]) | (lane == C_LANES[2])
    ones_lanes = jnp.where((lane >= HEAD_DIM) & (lane < HEAD_DIM + 3), 1.0, 0.0)
    q_hi, q_mid, q_lo = _split3(c_t[:N_HEADS])
    arow = lax.broadcasted_iota(jnp.int32, (AUG_ROWS, ts), 0)
    crow = lax.broadcasted_iota(jnp.int32, (HEAD_DIM - AUG_ROWS, ts), 0) + (HEAD_DIM + AUG_ROWS)
    q_ones = jnp.where((crow == C_LANES[0]) | (crow == C_LANES[1]) | (crow == C_LANES[2]),
                       1.0, 0.0).astype(bf16)
    for pair in range(N_HEADS // 2):
        lo_col = pair * LANES
        q_t = (proj[:, lo_col:lo_col + LANES] * q_scale).T
        k_pair = proj[:, d + lo_col:d + lo_col + LANES]
        k_swap = pltpu.roll(k_pair, HEAD_DIM, axis=1)
        v_t = proj[:, 2 * d + lo_col:2 * d + lo_col + LANES].T
        for e in range(2):
            hd = 2 * pair + e
            k_h = k_pair if e == 0 else k_swap
            c_h = pltpu.roll(c_pieces, C_LANES[0] - hd, axis=1)
            ka = jnp.where(lane < HEAD_DIM, k_h, jnp.where(is_c_lane, c_h, ones_lanes))
            ka_ref[:, hd * LANES:(hd + 1) * LANES] = ka.astype(bf16)
            aug = jnp.where(arow == 0, jnp.broadcast_to(q_hi[hd:hd + 1], (AUG_ROWS, ts)),
                  jnp.where(arow == 1, jnp.broadcast_to(q_mid[hd:hd + 1], (AUG_ROWS, ts)),
                  jnp.where(arow == 2, jnp.broadcast_to(q_lo[hd:hd + 1], (AUG_ROWS, ts)), 0.0)))
            base = hd * 2 * HEAD_DIM
            qta_ref[base:base + HEAD_DIM, :] = q_t[e * HEAD_DIM:(e + 1) * HEAD_DIM].astype(bf16)
            qta_ref[base + HEAD_DIM:base + HEAD_DIM + AUG_ROWS, :] = aug.astype(bf16)
            qta_ref[base + HEAD_DIM + AUG_ROWS:base + 2 * HEAD_DIM, :] = q_ones
            vbase = hd * VT_ROWS
            vta_ref[vbase:vbase + HEAD_DIM, :] = v_t[e * HEAD_DIM:(e + 1) * HEAD_DIM].astype(bf16)
            vta_ref[vbase + HEAD_DIM:vbase + VT_ROWS, :] = ones_blk
    z = proj[:, 3 * d:4 * d]
    gate_ref[...] = (z * jax.nn.sigmoid(z)).astype(bf16)


def _attn_in(x, g, w_in, layer, w_f, b_f, *, ts):
    s, d = x.shape
    kern = functools.partial(_attn_in_kernel, ts=ts, d=d)
    return pl.pallas_call(
        kern,
        grid=(s // ts,),
        in_specs=[
            pl.BlockSpec((ts, d), lambda i: (i, 0)),
            _const_spec((1, d)),
            _layer_spec((d, 4 * d), layer),
            _const_spec((d, LANES)),
            _const_spec((1, LANES)),
        ],
        out_specs=[
            pl.BlockSpec((N_HEADS * 2 * HEAD_DIM, ts), lambda i: (0, i)),
            pl.BlockSpec((ts, N_HEADS * LANES), lambda i: (i, 0)),
            pl.BlockSpec((N_HEADS * VT_ROWS, ts), lambda i: (0, i)),
            pl.BlockSpec((ts, d), lambda i: (i, 0)),
        ],
        out_shape=[
            jax.ShapeDtypeStruct((N_HEADS * 2 * HEAD_DIM, s), bf16),
            jax.ShapeDtypeStruct((s, N_HEADS * LANES), bf16),
            jax.ShapeDtypeStruct((N_HEADS * VT_ROWS, s), bf16),
            jax.ShapeDtypeStruct((s, d), bf16),
        ],
        scratch_shapes=[pltpu.VMEM((1, LANES), f32)],
        compiler_params=pltpu.CompilerParams(
            dimension_semantics=("arbitrary",), vmem_limit_bytes=VMEM_LIMIT_BYTES),
        name="attn_in",
    )(x, g, w_in, w_f, b_f)


def _fox_attn_kernel(qta_ref, ka_ref, vta_ref, ot_ref, *scratch, tq, tk):
    i = pl.program_id(1)
    nsub = tq // tk
    s_buf, p_buf, alpha_buf, cmax_buf = (scratch[k * nsub:(k + 1) * nsub] for k in range(4))
    m_ref, acc_ref, qa_ref = scratch[4 * nsub:]

    def column_max(s_g):
        m8 = jnp.max(s_g.reshape(s_g.shape[0] // SUBLANES, SUBLANES, LANES), axis=0)
        for sh in (4, 2, 1):
            m8 = jnp.maximum(m8, pltpu.roll(m8, sh, axis=0))
        return m8

    def scores(c, slot, col0=0, col1=tq):
        start = pl.multiple_of(c * tk, tk)
        s_t = jnp.dot(ka_ref[pl.ds(start, tk), :], qa_ref[:, col0:col1],
                      preferred_element_type=f32)
        for n0 in range(col0, col1, LANES):
            s_g = s_t[:, n0 - col0:n0 - col0 + LANES]
            cmax_buf[slot][:, n0:n0 + LANES] = column_max(s_g)
            s_buf[slot][:, n0:n0 + LANES] = s_g

    def softmax(slot, col0=0, col1=tq, diagonal=False):
        for g in range(col0 // LANES, col1 // LANES):
            cols = slice(g * LANES, (g + 1) * LANES)
            masked = diagonal and g * LANES < col0 + tk

            def visible(row0, rows):
                r = lax.broadcasted_iota(jnp.int32, (rows, LANES), 0) + row0
                col = lax.broadcasted_iota(jnp.int32, (rows, LANES), 1) + (g * LANES - col0)
                return r <= col

            if masked:
                s_g = s_buf[slot][:, cols]
                c_max = column_max(jnp.where(visible(0, tk), s_g, MASK_VALUE))
            else:
                c_max = cmax_buf[slot][:, cols]
            m_prev = m_ref[:, cols]
            m_new = jnp.maximum(m_prev, c_max)
            alpha_buf[slot][:, cols] = jnp.exp2(m_prev - m_new)
            m_ref[:, cols] = m_new
            for r0 in range(0, tk, BF16_ROWS):
                s_r = s_buf[slot][r0:r0 + BF16_ROWS, cols]
                if masked:
                    s_r = jnp.where(visible(r0, BF16_ROWS), s_r, MASK_VALUE)
                x = s_r.reshape(2, SUBLANES, LANES) - m_new[None]
                p_buf[slot][r0:r0 + BF16_ROWS, cols] = jnp.exp2(
                    x.reshape(BF16_ROWS, LANES).astype(bf16))

    def accumulate(c, slot, col0=0, col1=tq):
        start = pl.multiple_of(c * tk, tk)
        pv = jnp.dot(vta_ref[:, pl.ds(start, tk)], p_buf[slot][:, col0:col1],
                     preferred_element_type=f32)
        for n0 in range(col0, col1, MXU_COLS):
            cols = slice(n0, n0 + MXU_COLS)
            acc3 = acc_ref[:, cols].reshape(ACC_ROWS // SUBLANES, SUBLANES, MXU_COLS)
            pv3 = pv[:ACC_ROWS, n0 - col0:n0 - col0 + MXU_COLS].reshape(acc3.shape)
            acc3 = alpha_buf[slot][:, cols][None] * acc3 + pv3
            acc_ref[:, cols] = acc3.reshape(ACC_ROWS, MXU_COLS)

    m_ref[...] = jnp.full_like(m_ref, MASK_VALUE)
    acc_ref[:, :tq] = jnp.zeros((ACC_ROWS, tq), f32)
    p_buf[nsub - 1][:, :tq] = jnp.zeros((tk, tq), bf16)
    alpha_buf[nsub - 1][...] = jnp.ones_like(alpha_buf[nsub - 1])
    qa_ref[:, :tq] = qta_ref[...]
    scores(0, 0)

    def group(base):
        for u in range(nsub):
            c = base + u
            softmax(u)
            accumulate(jnp.maximum(c - 1, 0), (u - 1) % nsub)
            scores(c + 1, (u + 1) % nsub)

    odd = i & 1

    @pl.when(odd == 1)
    def _():
        group(0)

    def body(jj, carry):
        base = (odd + 2 * jj) * nsub
        group(base)
        group(base + nsub)
        return carry
    lax.fori_loop(0, i // 2, body, 0)

    first = i * nsub
    for d in range(nsub):
        if d + 1 < nsub:
            scores(first + d + 1, d + 1, (d + 1) * tk)
        softmax(d, d * tk, tq, diagonal=True)
        accumulate(jnp.maximum(first + d - 1, 0), (d - 1) % nsub, max(d - 1, 0) * tk)
    accumulate(first + nsub - 1, nsub - 1, (nsub - 1) * tk)

    acc = acc_ref[:, :tq]
    inv_l = 1.0 / acc[HEAD_DIM:HEAD_DIM + 1, :]
    ot_ref[...] = (acc[:HEAD_DIM, :] * inv_l).astype(bf16)


def _fox_attn(qta, ka, vta, *, tq, tk):
    s = ka.shape[0]
    nsub = tq // tk
    assert nsub >= 2
    kern = functools.partial(_fox_attn_kernel, tq=tq, tk=tk)
    return pl.pallas_call(
        kern,
        grid=(N_HEADS, s // tq),
        in_specs=[
            pl.BlockSpec((2 * HEAD_DIM, tq), lambda h, i: (h, i)),
            pl.BlockSpec((s, LANES), lambda h, i: (0, h)),
            pl.BlockSpec((VT_ROWS, s), lambda h, i: (h, 0)),
        ],
        out_specs=pl.BlockSpec((HEAD_DIM, tq), lambda h, i: (h, i)),
        out_shape=jax.ShapeDtypeStruct((N_HEADS * HEAD_DIM, s), bf16),
        scratch_shapes=(
            [pltpu.VMEM((tk, tq + PITCH_PAD), f32)] * nsub
            + [pltpu.VMEM((tk, tq + PITCH_PAD), bf16)] * nsub
            + [pltpu.VMEM((SUBLANES, tq), f32)] * nsub
            + [pltpu.VMEM((SUBLANES, tq), f32)] * nsub
            + [pltpu.VMEM((SUBLANES, tq), f32),
               pltpu.VMEM((ACC_ROWS, tq + PITCH_PAD), f32),
               pltpu.VMEM((2 * HEAD_DIM, tq + PITCH_PAD), bf16)]
        ),
        compiler_params=pltpu.CompilerParams(
            dimension_semantics=("arbitrary", "arbitrary"),
            vmem_limit_bytes=VMEM_LIMIT_BYTES),
        name="fox_attn",
    )(qta, ka, vta)


def _conv_in_kernel(x_ref, g_ref, w_ref, cw_ref, a_ref, tail_ref, *, ts, d):
    @pl.when(pl.program_id(0) == 0)
    def _():
        tail_ref[...] = jnp.zeros_like(tail_ref)

    h = _rmsnorm(x_ref[...], g_ref[...]).astype(bf16)
    proj = jnp.dot(h, w_ref[...], preferred_element_type=f32)
    gate_b = proj[:, 0:d]
    u = proj[:, d:2 * d] * proj[:, 2 * d:3 * d]
    z = proj[:, 3 * d:4 * d]

    rows = lax.broadcasted_iota(jnp.int32, (ts, d), 0)
    tail = tail_ref[...]
    prev2 = jnp.broadcast_to(tail[0:1, :], (ts, d))
    prev1 = jnp.broadcast_to(tail[1:2, :], (ts, d))
    u1 = jnp.where(rows == 0, prev1, pltpu.roll(u, 1, axis=0))
    u2 = jnp.where(rows == 0, prev2, jnp.where(rows == 1, prev1, pltpu.roll(u, 2, axis=0)))
    tail_ref[0:2, :] = u[ts - 2:ts, :]
    cw = cw_ref[...]
    y = u2 * cw[0:1, :] + u1 * cw[1:2, :] + u * cw[2:3, :]
    a_ref[...] = (gate_b * y * (z * jax.nn.sigmoid(z))).astype(bf16)


def _conv_in(x, g, w_in, layer, conv_w, *, ts):
    s, d = x.shape
    kern = functools.partial(_conv_in_kernel, ts=ts, d=d)
    return pl.pallas_call(
        kern,
        grid=(s // ts,),
        in_specs=[
            pl.BlockSpec((ts, d), lambda i: (i, 0)),
            _const_spec((1, d)),
            _layer_spec((d, 4 * d), layer),
            _const_spec((8, d)),
        ],
        out_specs=pl.BlockSpec((ts, d), lambda i: (i, 0)),
        out_shape=jax.ShapeDtypeStruct((s, d), bf16),
        scratch_shapes=[pltpu.VMEM((8, d), f32)],
        compiler_params=pltpu.CompilerParams(
            dimension_semantics=("arbitrary",), vmem_limit_bytes=VMEM_LIMIT_BYTES),
        name="conv_in",
    )(x, g, w_in, conv_w)


def _mix_out_kernel(*refs, transposed):
    if transposed:
        (x_ref, ot_ref, gate_ref, wo_ref, gpost_ref, gpe_ref, wg_ref, p_ref, wp_ref, o_ref) = refs
        a = (ot_ref[...].astype(f32).T * gate_ref[...].astype(f32)).astype(bf16)
    else:
        (x_ref, a_ref, wo_ref, gpost_ref, gpe_ref, wg_ref, p_ref, wp_ref, o_ref) = refs
        a = a_ref[...]
    y = jnp.dot(a, wo_ref[...], preferred_element_type=f32)
    x1 = x_ref[...] + _rmsnorm(y, gpost_ref[...])
    hg = _rmsnorm(x1, gpe_ref[...]).astype(bf16)
    gate = jax.nn.sigmoid(jnp.dot(hg, wg_ref[...], preferred_element_type=f32))
    pp = jnp.dot(p_ref[...].astype(bf16), wp_ref[...], preferred_element_type=f32)
    o_ref[...] = x1 + gate * pp


def _mix_out(x, acts, w_out, mixer, g_post, g_pe, w_gate, p, w_proj, layer, *, ts, transposed):
    s, d = x.shape
    ple = p.shape[-1]
    row_spec = pl.BlockSpec((ts, d), lambda i: (i, 0))
    if transposed:
        act_specs = [pl.BlockSpec((d, ts), lambda i: (0, i)), row_spec]
    else:
        act_specs = [row_spec]
    kern = functools.partial(_mix_out_kernel, transposed=transposed)
    return pl.pallas_call(
        kern,
        grid=(s // ts,),
        in_specs=[row_spec] + act_specs + [
            _layer_spec((d, d), mixer),
            _const_spec((1, d)),
            _const_spec((1, d)),
            _layer_spec((d, d), layer),
            pl.BlockSpec((None, None, ts, ple), lambda i: (layer, 0, i, 0)),
            _layer_spec((ple, d), layer),
        ],
        out_specs=row_spec,
        out_shape=jax.ShapeDtypeStruct((s, d), f32),
        compiler_params=pltpu.CompilerParams(
            dimension_semantics=("arbitrary",), vmem_limit_bytes=VMEM_LIMIT_BYTES),
        name="mix_out_t" if transposed else "mix_out",
    )(x, *acts, w_out, g_post, g_pe, w_gate, p, w_proj)


def _tile(s, want):
    return math.gcd(s, want)


def kernel(x, p, attn_w_in, attn_b_f, attn_w_out, conv_w_in, conv_kernel, conv_w_out,
           norm_pre, norm_post, pe_norm, pe_w_gate, pe_w_proj):
    b, s, d = x.shape
    assert b == 1 and d == N_HEADS * HEAD_DIM
    depth = p.shape[0]
    ts = _tile(s, 512)
    ts_wide = _tile(s, 1024)
    tq = _tile(s, 1024)
    tk = _tile(s, 256)
    xs = x.reshape(s, d)
    attn_w_in_b = attn_w_in.astype(bf16)
    attn_w_out_b = attn_w_out.astype(bf16)
    conv_w_in_b = conv_w_in.astype(bf16)
    conv_w_out_b = conv_w_out.astype(bf16)
    w_gate_b = pe_w_gate.astype(bf16)
    w_proj_b = pe_w_proj.astype(bf16)
    for i in range(depth):
        j = i // 2
        g_pre = norm_pre[i].reshape(1, d)
        g_post = norm_post[i].reshape(1, d)
        g_pe = pe_norm[i].reshape(1, d)
        if i % 2 == 0:
            w_f = jnp.pad(attn_w_in_b[j, :, 4 * d:], ((0, 0), (0, LANES - N_HEADS)))
            b_f = jnp.pad(attn_b_f[j], (0, LANES - N_HEADS)).reshape(1, LANES)
            qta, ka, vta, gate = _attn_in(xs, g_pre, attn_w_in_b, j, w_f, b_f, ts=ts)
            ot = _fox_attn(qta, ka, vta, tq=tq, tk=tk)
            xs = _mix_out(xs, (ot, gate), attn_w_out_b, j, g_post, g_pe,
                          w_gate_b, p, w_proj_b, i, ts=ts_wide, transposed=True)
        else:
            cw = jnp.pad(conv_kernel[j], ((0, 8 - CONV_TAPS), (0, 0)))
            a = _conv_in(xs, g_pre, conv_w_in_b, j, cw, ts=ts_wide)
            xs = _mix_out(xs, (a,), conv_w_out_b, j, g_post, g_pe,
                          w_gate_b, p, w_proj_b, i, ts=ts_wide, transposed=False)
    return xs.reshape(b, s, d)
```

```python
import functools
import math

import jax
import jax.numpy as jnp
from jax import lax
from jax.experimental import pallas as pl
from jax.experimental.pallas import tpu as pltpu

N_HEADS = 16
HEAD_DIM = 64
CONV_TAPS = 3
RMS_EPS = 1e-6
LOG2E = math.log2(math.e)

LANES = 128
SUBLANES = 8
BF16_ROWS = 2 * SUBLANES
MXU_COLS = 256
PITCH_PAD = LANES
AUG_ROWS = 16
VT_ROWS = HEAD_DIM + AUG_ROWS
ACC_ROWS = HEAD_DIM + 8
C_LANES = (80, 96, 112)
VMEM_LIMIT_BYTES = 56 * 1024 * 1024
MASK_VALUE = -(2.0 ** 100)

f32 = jnp.float32
bf16 = jnp.bfloat16


def _rmsnorm(x, g):
    ms = jnp.mean(x * x, axis=-1, keepdims=True)
    return x * lax.rsqrt(ms + RMS_EPS) * g


def _split3(c):
    hi = c.astype(bf16).astype(f32)
    r = c - hi
    mid = r.astype(bf16).astype(f32)
    lo = (r - mid).astype(bf16).astype(f32)
    return hi, mid, lo


def _log_sigmoid(x):
    return jnp.minimum(x, 0.0) - jnp.log1p(jnp.exp(-jnp.abs(x)))


def _const_spec(shape):
    return pl.BlockSpec(shape, lambda *_: (0,) * len(shape))


def _layer_spec(shape, layer):
    return pl.BlockSpec((None,) + shape, lambda *_: (layer,) + (0,) * len(shape))


def _attn_in_kernel(x_ref, g_ref, w_ref, wf_ref, bf_ref,
                    qta_ref, ka_ref, vta_ref, gate_ref, carry_ref, *, ts, d):
    @pl.when(pl.program_id(0) == 0)
    def _():
        carry_ref[...] = jnp.zeros_like(carry_ref)

    h = _rmsnorm(x_ref[...], g_ref[...]).astype(bf16)
    proj = jnp.dot(h, w_ref[...], preferred_element_type=f32)
    f_logit = jnp.dot(h, wf_ref[...], preferred_element_type=f32) + bf_ref[...]
    logf2 = _log_sigmoid(f_logit) * LOG2E

    rows = lax.broadcasted_iota(jnp.int32, (ts, LANES), 0)
    c = logf2
    shift = 1
    while shift < ts:
        c = c + jnp.where(rows >= shift, pltpu.roll(c, shift, axis=0), 0.0)
        shift *= 2
    c = c + carry_ref[...]
    carry_ref[...] = c[ts - 1:ts, :]
    c_t = c.T

    q_scale = LOG2E / math.sqrt(HEAD_DIM)
    lane = lax.broadcasted_iota(jnp.int32, (ts, LANES), 1)
    vrow = lax.broadcasted_iota(jnp.int32, (AUG_ROWS, ts), 0)
    ones_blk = jnp.where(vrow == 0, 1.0, 0.0).astype(bf16)

    n_hi, n_mid, n_lo = (-piece for piece in _split3(c))
    c_pieces = jnp.where(lane < N_HEADS, n_hi,
               jnp.where(lane < 2 * N_HEADS, pltpu.roll(n_mid, N_HEADS, axis=1),
                         pltpu.roll(n_lo, 2 * N_HEADS, axis=1)))
    is_c_lane = (lane == C_LANES[0]) | (lane == C_LANES[1]) | (lane == C_LANES[2])
    ones_lanes = jnp.where((lane >= HEAD_DIM) & (lane < HEAD_DIM + 3), 1.0, 0.0)
    q_hi, q_mid, q_lo = _split3(c_t[:N_HEADS])
    arow = lax.broadcasted_iota(jnp.int32, (AUG_ROWS, ts), 0)
    crow = lax.broadcasted_iota(jnp.int32, (HEAD_DIM - AUG_ROWS, ts), 0) + (HEAD_DIM + AUG_ROWS)
    q_ones = jnp.where((crow == C_LANES[0]) | (crow == C_LANES[1]) | (crow == C_LANES[2]),
                       1.0, 0.0).astype(bf16)
    for pair in range(N_HEADS // 2):
        lo_col = pair * LANES
        q_t = (proj[:, lo_col:lo_col + LANES] * q_scale).T
        k_pair = proj[:, d + lo_col:d + lo_col + LANES]
        k_swap = pltpu.roll(k_pair, HEAD_DIM, axis=1)
        v_t = proj[:, 2 * d + lo_col:2 * d + lo_col + LANES].T
        for e in range(2):
            hd = 2 * pair + e
            k_h = k_pair if e == 0 else k_swap
            c_h = pltpu.roll(c_pieces, C_LANES[0] - hd, axis=1)
            ka = jnp.where(lane < HEAD_DIM, k_h, jnp.where(is_c_lane, c_h, ones_lanes))
            ka_ref[:, hd * LANES:(hd + 1) * LANES] = ka.astype(bf16)
            aug = jnp.where(arow == 0, jnp.broadcast_to(q_hi[hd:hd + 1], (AUG_ROWS, ts)),
                  jnp.where(arow == 1, jnp.broadcast_to(q_mid[hd:hd + 1], (AUG_ROWS, ts)),
                  jnp.where(arow == 2, jnp.broadcast_to(q_lo[hd:hd + 1], (AUG_ROWS, ts)), 0.0)))
            base = hd * 2 * HEAD_DIM
            qta_ref[base:base + HEAD_DIM, :] = q_t[e * HEAD_DIM:(e + 1) * HEAD_DIM].astype(bf16)
            qta_ref[base + HEAD_DIM:base + HEAD_DIM + AUG_ROWS, :] = aug.astype(bf16)
            qta_ref[base + HEAD_DIM + AUG_ROWS:base + 2 * HEAD_DIM, :] = q_ones
            vbase = hd * VT_ROWS
            vta_ref[vbase:vbase + HEAD_DIM, :] = v_t[e * HEAD_DIM:(e + 1) * HEAD_DIM].astype(bf16)
            vta_ref[vbase + HEAD_DIM:vbase + VT_ROWS, :] = ones_blk
    z = proj[:, 3 * d:4 * d]
    gate_ref[...] = (z * jax.nn.sigmoid(z)).astype(bf16)


def _attn_in(x, g, w_in, layer, w_f, b_f, *, ts):
    s, d = x.shape
    kern = functools.partial(_attn_in_kernel, ts=ts, d=d)
    return pl.pallas_call(
        kern,
        grid=(s // ts,),
        in_specs=[
            pl.BlockSpec((ts, d), lambda i: (i, 0)),
            _const_spec((1, d)),
            _layer_spec((d, 4 * d), layer),
            _const_spec((d, LANES)),
            _const_spec((1, LANES)),
        ],
        out_specs=[
            pl.BlockSpec((N_HEADS * 2 * HEAD_DIM, ts), lambda i: (0, i)),
            pl.BlockSpec((ts, N_HEADS * LANES), lambda i: (i, 0)),
            pl.BlockSpec((N_HEADS * VT_ROWS, ts), lambda i: (0, i)),
            pl.BlockSpec((ts, d), lambda i: (i, 0)),
        ],
        out_shape=[
            jax.ShapeDtypeStruct((N_HEADS * 2 * HEAD_DIM, s), bf16),
            jax.ShapeDtypeStruct((s, N_HEADS * LANES), bf16),
            jax.ShapeDtypeStruct((N_HEADS * VT_ROWS, s), bf16),
            jax.ShapeDtypeStruct((s, d), bf16),
        ],
        scratch_shapes=[pltpu.VMEM((1, LANES), f32)],
        compiler_params=pltpu.CompilerParams(
            dimension_semantics=("arbitrary",), vmem_limit_bytes=VMEM_LIMIT_BYTES),
        name="attn_in",
    )(x, g, w_in, w_f, b_f)


def _fox_attn_kernel(qta_ref, ka_ref, vta_ref, ot_ref, *scratch, tq, tk):
    i = pl.program_id(1)
    nsub = tq // tk
    s_buf, p_buf, alpha_buf, cmax_buf = (scratch[k * nsub:(k + 1) * nsub] for k in range(4))
    m_ref, acc_ref, qa_ref = scratch[4 * nsub:]

    def column_max(s_g):
        m8 = jnp.max(s_g.reshape(s_g.shape[0] // SUBLANES, SUBLANES, LANES), axis=0)
        for sh in (4, 2, 1):
            m8 = jnp.maximum(m8, pltpu.roll(m8, sh, axis=0))
        return m8

    def scores(c, slot, col0=0, col1=tq):
        start = pl.multiple_of(c * tk, tk)
        s_t = jnp.dot(ka_ref[pl.ds(start, tk), :], qa_ref[:, col0:col1],
                      preferred_element_type=f32)
        for n0 in range(col0, col1, LANES):
            s_g = s_t[:, n0 - col0:n0 - col0 + LANES]
            cmax_buf[slot][:, n0:n0 + LANES] = column_max(s_g)
            s_buf[slot][:, n0:n0 + LANES] = s_g

    def softmax(slot, col0=0, col1=tq, diagonal=False):
        for g in range(col0 // LANES, col1 // LANES):
            cols = slice(g * LANES, (g + 1) * LANES)
            masked = diagonal and g * LANES < col0 + tk

            def visible(row0, rows):
                r = lax.broadcasted_iota(jnp.int32, (rows, LANES), 0) + row0
                col = lax.broadcasted_iota(jnp.int32, (rows, LANES), 1) + (g * LANES - col0)
                return r <= col

            if masked:
                s_g = s_buf[slot][:, cols]
                c_max = column_max(jnp.where(visible(0, tk), s_g, MASK_VALUE))
            else:
                c_max = cmax_buf[slot][:, cols]
            m_prev = m_ref[:, cols]
            m_new = jnp.maximum(m_prev, c_max)
            alpha_buf[slot][:, cols] = jnp.exp2(m_prev - m_new)
            m_ref[:, cols] = m_new
            for r0 in range(0, tk, BF16_ROWS):
                s_r = s_buf[slot][r0:r0 + BF16_ROWS, cols]
                if masked:
                    s_r = jnp.where(visible(r0, BF16_ROWS), s_r, MASK_VALUE)
                x = s_r.reshape(2, SUBLANES, LANES) - m_new[None]
                p_buf[slot][r0:r0 + BF16_ROWS, cols] = jnp.exp2(
                    x.reshape(BF16_ROWS, LANES).astype(bf16))

    def accumulate(c, slot, col0=0, col1=tq):
        start = pl.multiple_of(c * tk, tk)
        pv = jnp.dot(vta_ref[:, pl.ds(start, tk)], p_buf[slot][:, col0:col1],
                     preferred_element_type=f32)
        for n0 in range(col0, col1, MXU_COLS):
            cols = slice(n0, n0 + MXU_COLS)
            acc3 = acc_ref[:, cols].reshape(ACC_ROWS // SUBLANES, SUBLANES, MXU_COLS)
            pv3 = pv[:ACC_ROWS, n0 - col0:n0 - col0 + MXU_COLS].reshape(acc3.shape)
            acc3 = alpha_buf[slot][:, cols][None] * acc3 + pv3
            acc_ref[:, cols] = acc3.reshape(ACC_ROWS, MXU_COLS)

    m_ref[...] = jnp.full_like(m_ref, MASK_VALUE)
    acc_ref[:, :tq] = jnp.zeros((ACC_ROWS, tq), f32)
    p_buf[nsub - 1][:, :tq] = jnp.zeros((tk, tq), bf16)
    alpha_buf[nsub - 1][...] = jnp.ones_like(alpha_buf[nsub - 1])
    qa_ref[:, :tq] = qta_ref[...]
    scores(0, 0)

    def group(base):
        for u in range(nsub):
            c = base + u
            softmax(u)
            accumulate(jnp.maximum(c - 1, 0), (u - 1) % nsub)
            scores(c + 1, (u + 1) % nsub)

    one = i & 1
    two = i & 2

    @pl.when(one != 0)
    def _():
        group(0)

    @pl.when(two != 0)
    def _():
        group(one * nsub)
        group((one + 1) * nsub)

    def body(jj, carry):
        base = (one + two + 4 * jj) * nsub
        for k in range(4):
            group(base + k * nsub)
        return carry
    lax.fori_loop(0, i // 4, body, 0)

    first = i * nsub
    for d in range(nsub):
        if d + 1 < nsub:
            scores(first + d + 1, d + 1, (d + 1) * tk)
        softmax(d, d * tk, tq, diagonal=True)
        accumulate(jnp.maximum(first + d - 1, 0), (d - 1) % nsub, max(d - 1, 0) * tk)
    accumulate(first + nsub - 1, nsub - 1, (nsub - 1) * tk)

    acc = acc_ref[:, :tq]
    inv_l = 1.0 / acc[HEAD_DIM:HEAD_DIM + 1, :]
    ot_ref[...] = (acc[:HEAD_DIM, :] * inv_l).astype(bf16)


def _fox_attn(qta, ka, vta, *, tq, tk):
    s = ka.shape[0]
    nsub = tq // tk
    assert nsub >= 2
    kern = functools.partial(_fox_attn_kernel, tq=tq, tk=tk)
    return pl.pallas_call(
        kern,
        grid=(N_HEADS, s // tq),
        in_specs=[
            pl.BlockSpec((2 * HEAD_DIM, tq), lambda h, i: (h, i)),
            pl.BlockSpec((s, LANES), lambda h, i: (0, h)),
            pl.BlockSpec((VT_ROWS, s), lambda h, i: (h, 0)),
        ],
        out_specs=pl.BlockSpec((HEAD_DIM, tq), lambda h, i: (h, i)),
        out_shape=jax.ShapeDtypeStruct((N_HEADS * HEAD_DIM, s), bf16),
        scratch_shapes=(
            [pltpu.VMEM((tk, tq + PITCH_PAD), f32)] * nsub
            + [pltpu.VMEM((tk, tq + PITCH_PAD), bf16)] * nsub
            + [pltpu.VMEM((SUBLANES, tq), f32)] * nsub
            + [pltpu.VMEM((SUBLANES, tq), f32)] * nsub
            + [pltpu.VMEM((SUBLANES, tq), f32),
               pltpu.VMEM((ACC_ROWS, tq + PITCH_PAD), f32),
               pltpu.VMEM((2 * HEAD_DIM, tq + PITCH_PAD), bf16)]
        ),
        compiler_params=pltpu.CompilerParams(
            dimension_semantics=("arbitrary", "arbitrary"),
            vmem_limit_bytes=VMEM_LIMIT_BYTES),
        name="fox_attn",
    )(qta, ka, vta)


def _conv_in_kernel(x_ref, g_ref, w_ref, cw_ref, a_ref, tail_ref, *, ts, d):
    @pl.when(pl.program_id(0) == 0)
    def _():
        tail_ref[...] = jnp.zeros_like(tail_ref)

    h = _rmsnorm(x_ref[...], g_ref[...]).astype(bf16)
    proj = jnp.dot(h, w_ref[...], preferred_element_type=f32)
    gate_b = proj[:, 0:d]
    u = proj[:, d:2 * d] * proj[:, 2 * d:3 * d]
    z = proj[:, 3 * d:4 * d]

    rows = lax.broadcasted_iota(jnp.int32, (ts, d), 0)
    tail = tail_ref[...]
    prev2 = jnp.broadcast_to(tail[0:1, :], (ts, d))
    prev1 = jnp.broadcast_to(tail[1:2, :], (ts, d))
    u1 = jnp.where(rows == 0, prev1, pltpu.roll(u, 1, axis=0))
    u2 = jnp.where(rows == 0, prev2, jnp.where(rows == 1, prev1, pltpu.roll(u, 2, axis=0)))
    tail_ref[0:2, :] = u[ts - 2:ts, :]
    cw = cw_ref[...]
    y = u2 * cw[0:1, :] + u1 * cw[1:2, :] + u * cw[2:3, :]
    a_ref[...] = (gate_b * y * (z * jax.nn.sigmoid(z))).astype(bf16)


def _conv_in(x, g, w_in, layer, conv_w, *, ts):
    s, d = x.shape
    kern = functools.partial(_conv_in_kernel, ts=ts, d=d)
    return pl.pallas_call(
        kern,
        grid=(s // ts,),
        in_specs=[
            pl.BlockSpec((ts, d), lambda i: (i, 0)),
            _const_spec((1, d)),
            _layer_spec((d, 4 * d), layer),
            _const_spec((8, d)),
        ],
        out_specs=pl.BlockSpec((ts, d), lambda i: (i, 0)),
        out_shape=jax.ShapeDtypeStruct((s, d), bf16),
        scratch_shapes=[pltpu.VMEM((8, d), f32)],
        compiler_params=pltpu.CompilerParams(
            dimension_semantics=("arbitrary",), vmem_limit_bytes=VMEM_LIMIT_BYTES),
        name="conv_in",
    )(x, g, w_in, conv_w)


def _mix_out_kernel(*refs, transposed):
    if transposed:
        (x_ref, ot_ref, gate_ref, wo_ref, gpost_ref, gpe_ref, wg_ref, p_ref, wp_ref, o_ref) = refs
        a = (ot_ref[...].astype(f32).T * gate_ref[...].astype(f32)).astype(bf16)
    else:
        (x_ref, a_ref, wo_ref, gpost_ref, gpe_ref, wg_ref, p_ref, wp_ref, o_ref) = refs
        a = a_ref[...]
    y = jnp.dot(a, wo_ref[...], preferred_element_type=f32)
    x1 = x_ref[...] + _rmsnorm(y, gpost_ref[...])
    hg = _rmsnorm(x1, gpe_ref[...]).astype(bf16)
    gate = jax.nn.sigmoid(jnp.dot(hg, wg_ref[...], preferred_element_type=f32))
    pp = jnp.dot(p_ref[...].astype(bf16), wp_ref[...], preferred_element_type=f32)
    o_ref[...] = x1 + gate * pp


def _mix_out(x, acts, w_out, mixer, g_post, g_pe, w_gate, p, w_proj, layer, *, ts, transposed):
    s, d = x.shape
    ple = p.shape[-1]
    row_spec = pl.BlockSpec((ts, d), lambda i: (i, 0))
    if transposed:
        act_specs = [pl.BlockSpec((d, ts), lambda i: (0, i)), row_spec]
    else:
        act_specs = [row_spec]
    kern = functools.partial(_mix_out_kernel, transposed=transposed)
    return pl.pallas_call(
        kern,
        grid=(s // ts,),
        in_specs=[row_spec] + act_specs + [
            _layer_spec((d, d), mixer),
            _const_spec((1, d)),
            _const_spec((1, d)),
            _layer_spec((d, d), layer),
            pl.BlockSpec((None, None, ts, ple), lambda i: (layer, 0, i, 0)),
            _layer_spec((ple, d), layer),
        ],
        out_specs=row_spec,
        out_shape=jax.ShapeDtypeStruct((s, d), f32),
        compiler_params=pltpu.CompilerParams(
            dimension_semantics=("arbitrary",), vmem_limit_bytes=VMEM_LIMIT_BYTES),
        name="mix_out_t" if transposed else "mix_out",
    )(x, *acts, w_out, g_post, g_pe, w_gate, p, w_proj)


def _tile(s, want):
    return math.gcd(s, want)


def kernel(x, p, attn_w_in, attn_b_f, attn_w_out, conv_w_in, conv_kernel, conv_w_out,
           norm_pre, norm_post, pe_norm, pe_w_gate, pe_w_proj):
    b, s, d = x.shape
    assert b == 1 and d == N_HEADS * HEAD_DIM
    depth = p.shape[0]
    ts = _tile(s, 512)
    ts_wide = _tile(s, 1024)
    tq = _tile(s, 1024)
    tk = _tile(s, 256)
    xs = x.reshape(s, d)
    attn_w_in_b = attn_w_in.astype(bf16)
    attn_w_out_b = attn_w_out.astype(bf16)
    conv_w_in_b = conv_w_in.astype(bf16)
    conv_w_out_b = conv_w_out.astype(bf16)
    w_gate_b = pe_w_gate.astype(bf16)
    w_proj_b = pe_w_proj.astype(bf16)
    for i in range(depth):
        j = i // 2
        g_pre = norm_pre[i].reshape(1, d)
        g_post = norm_post[i].reshape(1, d)
        g_pe = pe_norm[i].reshape(1, d)
        if i % 2 == 0:
            w_f = jnp.pad(attn_w_in_b[j, :, 4 * d:], ((0, 0), (0, LANES - N_HEADS)))
            b_f = jnp.pad(attn_b_f[j], (0, LANES - N_HEADS)).reshape(1, LANES)
            qta, ka, vta, gate = _attn_in(xs, g_pre, attn_w_in_b, j, w_f, b_f, ts=ts)
            ot = _fox_attn(qta, ka, vta, tq=tq, tk=tk)
            xs = _mix_out(xs, (ot, gate), attn_w_out_b, j, g_post, g_pe,
                          w_gate_b, p, w_proj_b, i, ts=ts_wide, transposed=True)
        else:
            cw = jnp.pad(conv_kernel[j], ((0, 8 - CONV_TAPS), (0, 0)))
            a = _conv_in(xs, g_pre, conv_w_in_b, j, cw, ts=ts_wide)
            xs = _mix_out(xs, (a,), conv_w_out_b, j, g_post, g_pe,
                          w_gate_b, p, w_proj_b, i, ts=ts_wide, transposed=False)
    return xs.reshape(b, s, d)
```
